```python
import math
import jax, jax.numpy as jnp
from jax import lax
import numpy as np

D_MODEL = 2048
BATCH = 1
SEQ = 8192
DEPTH = 4

GRID_W = 64
CTX_LEN = 256

HEAD_DIM = 128
A_HEADS = 8
A_KV_HEADS = 2
A_GROUP = A_HEADS // A_KV_HEADS
B_HEADS = 4
B_QK_DIM = HEAD_DIM // 2
B_V_DIM = HEAD_DIM
C_GROUPS = 4
C_DIM = HEAD_DIM
CHUNK = 128
Q_BLOCK = 128
ROPE_THETA = 10000.0

A_Q = A_HEADS * HEAD_DIM
A_KV = A_KV_HEADS * HEAD_DIM
B_QK = B_HEADS * 2 * B_QK_DIM
B_V = B_HEADS * B_V_DIM
C_U = C_GROUPS * C_DIM
D_IN = A_Q + 2 * A_KV + 2 * B_QK + B_V + 2 * C_U
D_MIX = A_Q + B_V + C_U
PROJ_SPLITS = (A_Q, A_Q + A_KV, A_Q + 2 * A_KV, A_Q + 2 * A_KV + B_QK, A_Q + 2 * A_KV + 2 * B_QK,
               A_Q + 2 * A_KV + 2 * B_QK + B_V, A_Q + 2 * A_KV + 2 * B_QK + B_V + C_U)

N_EXPERTS = 16
N_EXPERT_GROUPS = 4
EXPERTS_PER_GROUP = N_EXPERTS // N_EXPERT_GROUPS
TOP_K = 2
D_EXPERT = 1024
EXPERT_BLOCK = 128

DEEPNORM_ALPHA = (2 * DEPTH) ** 0.25
DEEPNORM_BETA = (8 * DEPTH) ** -0.25
NORM_EPS = 1e-6

kernel_name = "hybrid_headgroup_dit_moe_trunk"


def _rms_norm(x, g):
    xf = x.astype(jnp.float32)
    y = xf * lax.rsqrt(jnp.mean(xf * xf, axis=-1, keepdims=True) + NORM_EPS)
    return y.astype(x.dtype) * g


def _standardize(x):
    xf = x.astype(jnp.float32)
    mu = jnp.mean(xf, axis=-1, keepdims=True)
    var = jnp.mean(jnp.square(xf - mu), axis=-1, keepdims=True)
    return ((xf - mu) * lax.rsqrt(var + NORM_EPS)).astype(x.dtype)


def _layer_norm(x, g, b):
    return _standardize(x) * g + b


def _modulate(h, shift, scale):
    return h * (1 + scale) + shift


def _rope_tables(rows, cols, dim):
    n_freq = dim // 4
    inv = ROPE_THETA ** (-jnp.arange(n_freq, dtype=jnp.float32) / n_freq)
    ang_r = rows[:, None] * inv[None, :]
    ang_c = cols[:, None] * inv[None, :]
    return (jnp.cos(ang_r), jnp.sin(ang_r), jnp.cos(ang_c), jnp.sin(ang_c))


def _rotate(x, cos, sin):
    x1, x2 = jnp.split(x, 2, axis=-1)
    cos = cos.astype(x.dtype)
    sin = sin.astype(x.dtype)
    return jnp.concatenate([x1 * cos - x2 * sin, x2 * cos + x1 * sin], axis=-1)


def _axial_rope(x, tab):
    cos_r, sin_r, cos_c, sin_c = tab
    xr, xc = jnp.split(x, 2, axis=-1)
    return jnp.concatenate([_rotate(xr, cos_r, sin_r), _rotate(xc, cos_c, sin_c)], axis=-1)


def _latent_attention(q_rot, q_free, k_lat, v_lat, k_ctx, v_ctx, scale):
    B, H, G, S, d = q_rot.shape
    nb = S // Q_BLOCK
    k_lat_t = jnp.swapaxes(k_lat, -1, -2)
    k_ctx_t = jnp.swapaxes(k_ctx, -1, -2)

    def to_blocks(q):
        return jnp.moveaxis(q.reshape(B, H, G, nb, Q_BLOCK, d), 3, 0)

    def one_block(qs):
        qr, qf = qs
        s = jnp.concatenate([jnp.matmul(qr, k_lat_t), jnp.matmul(qf, k_ctx_t)], axis=-1)
        p = jax.nn.softmax(s.astype(jnp.float32) * scale, axis=-1).astype(v_lat.dtype)
        return jnp.matmul(p[..., :S], v_lat) + jnp.matmul(p[..., S:], v_ctx)

    out = lax.map(one_block, (to_blocks(q_rot), to_blocks(q_free)))
    return jnp.moveaxis(out, 0, 3).reshape(B, H, G, S, v_lat.shape[-1])


def _context_attention(q, k, v, scale):
    s = jnp.matmul(q, jnp.swapaxes(k, -1, -2)).astype(jnp.float32) * scale
    return jnp.matmul(jax.nn.softmax(s, axis=-1).astype(v.dtype), v)


def _split_heads(p):
    B, T, _ = p.shape
    aq, ak, av, bq, bk, bv, cu, cv = jnp.split(p, PROJ_SPLITS, axis=-1)
    aq = aq.reshape(B, T, A_KV_HEADS, A_GROUP, HEAD_DIM).transpose(0, 2, 3, 1, 4)
    ak = ak.reshape(B, T, A_KV_HEADS, 1, HEAD_DIM).transpose(0, 2, 3, 1, 4)
    av = av.reshape(B, T, A_KV_HEADS, 1, HEAD_DIM).transpose(0, 2, 3, 1, 4)
    bq = bq.reshape(B, T, B_HEADS, 2, B_QK_DIM).transpose(0, 2, 3, 1, 4)
    bk = bk.reshape(B, T, B_HEADS, 2, B_QK_DIM).transpose(0, 2, 3, 1, 4)
    bv = bv.reshape(B, T, B_HEADS, 1, B_V_DIM).transpose(0, 2, 3, 1, 4)
    cu = cu.reshape(B, T, C_GROUPS, C_DIM)
    cv = cv.reshape(B, T, C_GROUPS, C_DIM)
    return aq, ak, av, bq, bk, bv, cu, cv


def _merge_gqa(o, g):
    B, H, G, T, d = o.shape
    o = o.transpose(0, 3, 1, 2, 4).reshape(B, T, H * G, d)
    return _rms_norm(o, g).reshape(B, T, H * G * d)


def _merge_diff(o, lam, g, lam_init):
    o = o[:, :, 0] - lam * o[:, :, 1]
    B, H, T, dv = o.shape
    o = o.transpose(0, 2, 1, 3)
    return (_rms_norm(o, g) * (1.0 - lam_init)).reshape(B, T, H * dv)


def _chunk_gating(u, v, w_s, b_s, g_out):
    B, T, G, d = u.shape
    u = jax.nn.gelu(u, approximate=False)
    v = _standardize(jax.nn.gelu(v, approximate=False))
    vc = v.reshape(B, T // CHUNK, CHUNK, G, d)
    mixed = jnp.einsum('gpq,bnqgd->bnpgd', w_s, vc) + b_s.T[:, :, None]
    y = u * mixed.reshape(B, T, G, d)
    return _rms_norm(y, g_out).reshape(B, T, G * d)


def _hybrid_mixer(h_lat, h_ctx, w_in, w_out, a_q_norm, a_k_norm, a_out_norm, b_lambda, b_out_norm,
                  c_spatial, c_spatial_bias, c_out_norm, rope_a, rope_b, lam_init, with_ctx_out):
    aq, ak, av, bq, bk, bv, cu, cv = _split_heads(h_lat @ w_in)
    aq_c, ak_c, av_c, bq_c, bk_c, bv_c, cu_c, cv_c = _split_heads(h_ctx @ w_in)
    scale_a = HEAD_DIM ** -0.5
    scale_b = B_QK_DIM ** -0.5

    aq = _rms_norm(aq, a_q_norm)
    ak = _rms_norm(ak, a_k_norm)
    ak_c = _rms_norm(ak_c, a_k_norm)
    o_a = _latent_attention(_axial_rope(aq, rope_a), aq, _axial_rope(ak, rope_a), av, ak_c, av_c, scale_a)

    lam_f = b_lambda.astype(jnp.float32)
    lam = (jnp.exp(jnp.sum(lam_f[0] * lam_f[1])) - jnp.exp(jnp.sum(lam_f[2] * lam_f[3])) + lam_init).astype(h_lat.dtype)
    o_b = _latent_attention(_axial_rope(bq, rope_b), bq, _axial_rope(bk, rope_b), bv, bk_c, bv_c, scale_b)

    y_lat = jnp.concatenate([
        _merge_gqa(o_a, a_out_norm),
        _merge_diff(o_b, lam, b_out_norm, lam_init),
        _chunk_gating(cu, cv, c_spatial, c_spatial_bias, c_out_norm),
    ], axis=-1) @ w_out
    if not with_ctx_out:
        return y_lat, None

    aq_c = _rms_norm(aq_c, a_q_norm)
    y_ctx = jnp.concatenate([
        _merge_gqa(_context_attention(aq_c, ak_c, av_c, scale_a), a_out_norm),
        _merge_diff(_context_attention(bq_c, bk_c, bv_c, scale_b), lam, b_out_norm, lam_init),
        _chunk_gating(cu_c, cv_c, c_spatial, c_spatial_bias, c_out_norm),
    ], axis=-1) @ w_out
    return y_lat, y_ctx


def _moe(h, w_router, router_bias, w_gate, w_up, w_down):
    T, D = h.shape
    scores = jax.nn.sigmoid((h @ w_router).astype(jnp.float32))
    biased = scores + router_bias.astype(jnp.float32)
    grouped = biased.reshape(T, N_EXPERT_GROUPS, EXPERTS_PER_GROUP)
    group_score = lax.top_k(grouped, 2)[0].sum(axis=-1)
    grp = jnp.argmax(group_score, axis=-1)
    in_grp = jnp.take_along_axis(grouped, grp[:, None, None], axis=1)[:, 0]
    _, local = lax.top_k(in_grp, TOP_K)
    eid = grp[:, None] * EXPERTS_PER_GROUP + local
    gate = jnp.take_along_axis(scores, eid, axis=1)
    gate = gate / jnp.sum(gate, axis=-1, keepdims=True)

    n_assign = T * TOP_K
    flat_e = eid.reshape(-1)
    flat_t = jnp.repeat(jnp.arange(T), TOP_K)
    flat_g = gate.reshape(-1)
    order = jnp.argsort(flat_e)
    e_s, t_s, g_s = flat_e[order], flat_t[order], flat_g[order]
    counts = jnp.bincount(flat_e, length=N_EXPERTS)
    padded = (counts + EXPERT_BLOCK - 1) // EXPERT_BLOCK * EXPERT_BLOCK
    ends = jnp.cumsum(padded)
    dest = (ends - padded)[e_s] + jnp.arange(n_assign) - (jnp.cumsum(counts) - counts)[e_s]
    n_blocks = (n_assign + N_EXPERTS * (EXPERT_BLOCK - 1) + EXPERT_BLOCK - 1) // EXPERT_BLOCK
    buf = jnp.zeros((n_blocks * EXPERT_BLOCK, D), h.dtype).at[dest].set(h[t_s])
    block_e = jnp.minimum(jnp.searchsorted(ends, jnp.arange(n_blocks) * EXPERT_BLOCK, side='right'), N_EXPERTS - 1)

    def run_block(args):
        xb, e = args
        return (jax.nn.silu(xb @ w_gate[e]) * (xb @ w_up[e])) @ w_down[e]

    out = lax.map(run_block, (buf.reshape(n_blocks, EXPERT_BLOCK, D), block_e)).reshape(-1, D)
    return jax.ops.segment_sum(out[dest] * g_s[:, None].astype(out.dtype), t_s, num_segments=T)


def setup_inputs(seed: int = 0) -> dict:
    key = jax.random.key(seed)
    ks = jax.random.split(key, 25)
    L, D, E, F = DEPTH, D_MODEL, N_EXPERTS, D_EXPERT

    def nrm(k, shape, s):
        return jax.random.normal(k, shape, jnp.float32) * s

    return {
        "x": nrm(ks[0], (BATCH, SEQ, D), 1.0),
        "c": nrm(ks[1], (BATCH, D), 1.0),
        "ctx": nrm(ks[2], (BATCH, CTX_LEN, D), 1.0),
        "c_ctx": nrm(ks[3], (D,), 1.0),
        "w_ada": nrm(ks[4], (L, D, 6 * D), 0.5 * D ** -0.5),
        "b_ada": nrm(ks[5], (L, 6 * D), 0.02),
        "w_in": nrm(ks[6], (L, D, D_IN), D ** -0.5),
        "w_out": nrm(ks[7], (L, D_MIX, D), DEEPNORM_BETA * D_MIX ** -0.5),
        "a_q_norm": 1.0 + nrm(ks[8], (L, HEAD_DIM), 0.02),
        "a_k_norm": 1.0 + nrm(ks[9], (L, HEAD_DIM), 0.02),
        "a_out_norm": 1.0 + nrm(ks[10], (L, HEAD_DIM), 0.02),
        "b_lambda": nrm(ks[11], (L, 4, B_QK_DIM), 0.1),
        "b_out_norm": 1.0 + nrm(ks[12], (L, B_V_DIM), 0.02),
        "c_spatial": nrm(ks[13], (L, C_GROUPS, CHUNK, CHUNK), CHUNK ** -0.5),
        "c_spatial_bias": 1.0 + nrm(ks[14], (L, C_GROUPS, CHUNK), 0.02),
        "c_out_norm": 1.0 + nrm(ks[15], (L, C_DIM), 0.02),
        "ln1_g": 1.0 + nrm(ks[16], (L, D), 0.02),
        "ln1_b": nrm(ks[17], (L, D), 0.02),
        "ln2_g": 1.0 + nrm(ks[18], (L, D), 0.02),
        "ln2_b": nrm(ks[19], (L, D), 0.02),
        "w_router": nrm(ks[20], (D, E), D ** -0.5),
        "router_bias": nrm(ks[21], (E,), 0.01),
        "w_gate": nrm(ks[22], (L, E, D, F), D ** -0.5),
        "w_up": nrm(ks[23], (L, E, D, F), D ** -0.5),
        "w_down": nrm(ks[24], (L, E, F, D), DEEPNORM_BETA * F ** -0.5),
    }


def reference(x, c, ctx, c_ctx, w_ada, b_ada, w_in, w_out, a_q_norm, a_k_norm, a_out_norm, b_lambda,
              b_out_norm, c_spatial, c_spatial_bias, c_out_norm, ln1_g, ln1_b, ln2_g, ln2_b,
              w_router, router_bias, w_gate, w_up, w_down):
    B, S, D = x.shape
    C = ctx.shape[1]
    n_rows = S // GRID_W
    rows = jnp.repeat(jnp.arange(n_rows, dtype=jnp.float32), GRID_W)
    cols = (jnp.arange(n_rows * GRID_W) % GRID_W).astype(jnp.float32)
    rope_a = _rope_tables(rows, cols, HEAD_DIM)
    rope_b = _rope_tables(rows, cols, B_QK_DIM)
    c_act = jax.nn.silu(c)
    cctx_act = jax.nn.silu(c_ctx)
    h = ctx
    for l in range(DEPTH):
        last = l == DEPTH - 1
        lam_init = 0.8 - 0.6 * math.exp(-0.3 * l)
        m_lat = (c_act @ w_ada[l] + b_ada[l])[:, None, :]
        m_ctx = cctx_act @ w_ada[l] + b_ada[l]
        sh1, sc1, g1, sh2, sc2, g2 = jnp.split(m_lat, 6, axis=-1)
        sh1c, sc1c, g1c, sh2c, sc2c, g2c = jnp.split(m_ctx, 6, axis=-1)

        y_lat, y_ctx = _hybrid_mixer(
            _modulate(x, sh1, sc1), _modulate(h, sh1c, sc1c), w_in[l], w_out[l],
            a_q_norm[l], a_k_norm[l], a_out_norm[l], b_lambda[l], b_out_norm[l],
            c_spatial[l], c_spatial_bias[l], c_out_norm[l], rope_a, rope_b, lam_init, not last)
        x = _layer_norm(DEEPNORM_ALPHA * x + g1 * y_lat, ln1_g[l], ln1_b[l])

        if last:
            y = _moe(_modulate(x, sh2, sc2).reshape(B * S, D), w_router, router_bias, w_gate[l], w_up[l], w_down[l])
            x = _layer_norm(DEEPNORM_ALPHA * x + g2 * y.reshape(B, S, D), ln2_g[l], ln2_b[l])
        else:
            h = _layer_norm(DEEPNORM_ALPHA * h + g1c * y_ctx, ln1_g[l], ln1_b[l])
            tokens = jnp.concatenate([_modulate(h, sh2c, sc2c).reshape(B * C, D),
                                      _modulate(x, sh2, sc2).reshape(B * S, D)], axis=0)
            y = _moe(tokens, w_router, router_bias, w_gate[l], w_up[l], w_down[l])
            h = _layer_norm(DEEPNORM_ALPHA * h + g2c * y[:B * C].reshape(B, C, D), ln2_g[l], ln2_b[l])
            x = _layer_norm(DEEPNORM_ALPHA * x + g2 * y[B * C:].reshape(B, S, D), ln2_g[l], ln2_b[l])
    return x
```

```python
import functools
import math

import jax
import jax.numpy as jnp
from jax import lax
from jax.experimental import pallas as pl
from jax.experimental.pallas import tpu as pltpu

F32 = jnp.float32
BF16 = jnp.bfloat16

D_MODEL = 2048
SEQ = 8192
CTX = 256
TOK = CTX + SEQ
DEPTH = 4
GRID_W = 64

HEAD_DIM = 128
A_HEADS = 8
A_KV_HEADS = 2
A_GROUP = A_HEADS // A_KV_HEADS
B_HEADS = 4
B_QK_DIM = 64
C_GROUPS = 4
CHUNK = 128
ROPE_THETA = 10000.0

A_Q = A_HEADS * HEAD_DIM
A_KV = A_KV_HEADS * HEAD_DIM
B_QK = B_HEADS * HEAD_DIM
B_V = B_HEADS * HEAD_DIM
C_U = C_GROUPS * HEAD_DIM
D_IN = A_Q + 2 * A_KV + 2 * B_QK + B_V + 2 * C_U
D_MIX = A_Q + B_V + C_U
OFF_AQ = 0
OFF_AK = OFF_AQ + A_Q
OFF_AV = OFF_AK + A_KV
OFF_BQ = OFF_AV + A_KV
OFF_BK = OFF_BQ + B_QK
OFF_BV = OFF_BK + B_QK
OFF_CU = OFF_BV + B_V
OFF_CV = OFF_CU + C_U

N_EXPERTS = 16
N_EXPERT_GROUPS = 4
EXPERTS_PER_GROUP = 4
TOP_K = 2
D_EXPERT = 1024

DEEPNORM_ALPHA = (2 * DEPTH) ** 0.25
NORM_EPS = 1e-6

LANES = 128
VMEM_LIMIT = 60 * 1024 * 1024

TM = 256
N_TILES = TOK // TM
ADA_TN = 512
ATT_CK = 512
ATT_TQ_A = 128
ATT_TQ_B = 256
MOE_BM = 256
MOE_NB = (TOK * TOP_K + N_EXPERTS * (MOE_BM - 1) + MOE_BM - 1) // MOE_BM
MOE_ROWS = MOE_NB * MOE_BM

NT_DIMS = (((1,), (1,)), ((), ()))


def _rms(x, g):
    return x * lax.rsqrt(jnp.mean(x * x, axis=-1, keepdims=True) + NORM_EPS) * g


def _gelu(x):
    return 0.5 * x * (1.0 + lax.erf(x * (2.0 ** -0.5)))


def _standardize(x):
    mu = jnp.mean(x, axis=-1, keepdims=True)
    xc = x - mu
    var = jnp.mean(xc * xc, axis=-1, keepdims=True)
    return xc * lax.rsqrt(var + NORM_EPS)


def _ada_kernel(c_ref, w_ref, b_ref, o_ref):
    for r in range(2):
        cv = c_ref[r]
        act = cv * jax.nn.sigmoid(cv)
        for j in range(ADA_TN // LANES):
            cols = slice(j * LANES, (j + 1) * LANES)
            o_ref[0, r:r + 1, cols] = jnp.sum(w_ref[0, :, cols] * act, axis=0, keepdims=True) + b_ref[0, :, cols]


def _ada_call(c_rep, w_ada, b_ada):
    n_out = w_ada.shape[-1]
    return pl.pallas_call(
        _ada_kernel,
        grid=(DEPTH, n_out // ADA_TN),
        in_specs=[
            pl.BlockSpec((2, D_MODEL, LANES), lambda l, n: (0, 0, 0)),
            pl.BlockSpec((1, D_MODEL, ADA_TN), lambda l, n: (l, 0, n)),
            pl.BlockSpec((1, 1, ADA_TN), lambda l, n: (l, 0, n)),
        ],
        out_specs=pl.BlockSpec((1, 2, ADA_TN), lambda l, n: (l, 0, n)),
        out_shape=jax.ShapeDtypeStruct((DEPTH, 2, n_out), F32),
        compiler_params=pltpu.CompilerParams(dimension_semantics=("arbitrary", "arbitrary"),
                                             vmem_limit_bytes=VMEM_LIMIT),
        name="ada_ln",
    )(c_rep, w_ada, b_ada.reshape(DEPTH, 1, n_out))


def _rope(x, cos, sin_signed, half, lane):
    fwd = pltpu.roll(x, LANES - half, 1)
    bwd = pltpu.roll(x, half, 1)
    partner = jnp.where((lane & (2 * half - 1)) < half, fwd, bwd)
    return x * cos + partner * sin_signed


def _inproj_kernel(x_ref, mod_ref, w_ref, cosa_ref, sina_ref, cosb_ref, sinb_ref, gq_ref, gk_ref,
                   ws_ref, bs_ref, gc_ref,
                   qar_ref, qaf_ref, ka_ref, va_ref, qbr_ref, qbf_ref, kb_ref, vb_ref, yc_ref):
    x = x_ref[...]
    h = (x * (1.0 + mod_ref[0, 1:2, :]) + mod_ref[0, 0:1, :]).astype(BF16)
    lane = lax.broadcasted_iota(jnp.int32, (TM, LANES), 1)
    cosa, sina = cosa_ref[...], sina_ref[...]
    cosb, sinb = cosb_ref[...], sinb_ref[...]
    scale_a = HEAD_DIM ** -0.5
    scale_b = B_QK_DIM ** -0.5

    def proj(col):
        p = jnp.dot(h, w_ref[:, col:col + 2 * LANES], preferred_element_type=F32)
        return p[:, :LANES], p[:, LANES:]

    for j2 in range(A_HEADS // 2):
        for j, q in zip((2 * j2, 2 * j2 + 1), proj(OFF_AQ + j2 * 2 * LANES)):
            qn = _rms(q, gq_ref[...]) * scale_a
            qaf_ref[j] = qn.astype(BF16)
            qar_ref[j] = _rope(qn, cosa, sina, 32, lane).astype(BF16)
    for j, k in enumerate(proj(OFF_AK)):
        ka_ref[j] = _rope(_rms(k, gk_ref[...]), cosa, sina, 32, lane).astype(BF16)
    for j, v in enumerate(proj(OFF_AV)):
        va_ref[j] = v.astype(BF16)
    for j2 in range(B_HEADS // 2):
        for j, q in zip((2 * j2, 2 * j2 + 1), proj(OFF_BQ + j2 * 2 * LANES)):
            qs = q * scale_b
            qrot = _rope(qs, cosb, sinb, 16, lane)
            first = lane < B_QK_DIM
            qbf_ref[2 * j] = jnp.where(first, qs, 0.0).astype(BF16)
            qbf_ref[2 * j + 1] = jnp.where(first, 0.0, qs).astype(BF16)
            qbr_ref[2 * j] = jnp.where(first, qrot, 0.0).astype(BF16)
            qbr_ref[2 * j + 1] = jnp.where(first, 0.0, qrot).astype(BF16)
    for j2 in range(B_HEADS // 2):
        for j, k in zip((2 * j2, 2 * j2 + 1), proj(OFF_BK + j2 * 2 * LANES)):
            kb_ref[j] = _rope(k, cosb, sinb, 16, lane).astype(BF16)
    for j2 in range(B_HEADS // 2):
        for j, v in zip((2 * j2, 2 * j2 + 1), proj(OFF_BV + j2 * 2 * LANES)):
            vb_ref[j] = v.astype(BF16)
    for j2 in range(C_GROUPS // 2):
        us = proj(OFF_CU + j2 * 2 * LANES)
        vs = proj(OFF_CV + j2 * 2 * LANES)
        for g, u, v in zip((2 * j2, 2 * j2 + 1), us, vs):
            u = _gelu(u)
            v = _standardize(_gelu(v)).astype(BF16)
            for c in range(TM // CHUNK):
                rows = slice(c * CHUNK, (c + 1) * CHUNK)
                mixed = jnp.dot(ws_ref[g], v[rows], preferred_element_type=F32) + bs_ref[g]
                yc_ref[rows, g * LANES:(g + 1) * LANES] = _rms(u[rows] * mixed, gc_ref[...]).astype(BF16)


def _inproj_call(xs, mod, w_in, tabs, gq, gk, ws, bs, gc):
    def heads(n):
        return (jax.ShapeDtypeStruct((n, TOK, HEAD_DIM), BF16),
                pl.BlockSpec((n, TM, HEAD_DIM), lambda i: (0, i, 0)))

    outs = [heads(A_HEADS), heads(A_HEADS), heads(A_KV_HEADS), heads(A_KV_HEADS),
            heads(2 * B_HEADS), heads(2 * B_HEADS), heads(B_HEADS), heads(B_HEADS),
            (jax.ShapeDtypeStruct((TOK, C_U), BF16), pl.BlockSpec((TM, C_U), lambda i: (i, 0)))]
    tab_spec = pl.BlockSpec((TM, LANES), lambda i: (i, 0))
    vec_spec = pl.BlockSpec((1, LANES), lambda i: (0, 0))
    return pl.pallas_call(
        _inproj_kernel,
        grid=(N_TILES,),
        in_specs=[
            pl.BlockSpec((TM, D_MODEL), lambda i: (i, 0)),
            pl.BlockSpec((1, 6, D_MODEL), lambda i: (jnp.minimum(i, 1), 0, 0)),
            pl.BlockSpec((D_MODEL, D_IN), lambda i: (0, 0), pipeline_mode=pl.Buffered(1)),
            tab_spec, tab_spec, tab_spec, tab_spec, vec_spec, vec_spec,
            pl.BlockSpec((C_GROUPS, CHUNK, CHUNK), lambda i: (0, 0, 0)),
            pl.BlockSpec((C_GROUPS, CHUNK, LANES), lambda i: (0, 0, 0)),
            vec_spec,
        ],
        out_specs=[o[1] for o in outs],
        out_shape=[o[0] for o in outs],
        compiler_params=pltpu.CompilerParams(dimension_semantics=("arbitrary",), vmem_limit_bytes=VMEM_LIMIT),
        name="in_proj",
    )(xs, mod, w_in, *tabs, gq, gk, ws, bs, gc)


def _attn_kernel(*refs, groups, tq, diff, lam_init):
    if diff:
        qr_ref, qf_ref, k_ref, v_ref, lam_ref, g_ref, o_ref, m_sc, l_sc, acc_sc = refs
    else:
        qr_ref, qf_ref, k_ref, v_ref, g_ref, o_ref, m_sc, l_sc, acc_sc = refs
    i = pl.program_id(1)
    rows = groups * tq

    qf = qf_ref[...].reshape(rows, HEAD_DIM)
    s = lax.dot_general(qf, k_ref[0, 0:CTX, :], NT_DIMS, preferred_element_type=F32)
    m0 = jnp.max(s, axis=-1, keepdims=True)
    p = jnp.exp(s - m0)
    m_sc[...] = m0
    l_sc[...] = jnp.sum(p, axis=-1, keepdims=True)
    acc_sc[...] = jnp.dot(p.astype(BF16), v_ref[0, 0:CTX, :], preferred_element_type=F32)

    @pl.when(i >= CTX // tq)
    def _latent_keys():
        qr = qr_ref[...].reshape(rows, HEAD_DIM)

        def body(j, carry):
            off = pl.multiple_of(CTX + j * ATT_CK, CTX)
            s = lax.dot_general(qr, k_ref[0, pl.ds(off, ATT_CK), :], NT_DIMS, preferred_element_type=F32)
            m_prev = m_sc[...]
            m_new = jnp.maximum(m_prev, jnp.max(s, axis=-1, keepdims=True))
            alpha = jnp.exp(m_prev - m_new)
            p = jnp.exp(s - m_new)
            l_sc[...] = alpha * l_sc[...] + jnp.sum(p, axis=-1, keepdims=True)
            acc_sc[...] = alpha * acc_sc[...] + jnp.dot(p.astype(BF16), v_ref[0, pl.ds(off, ATT_CK), :],
                                                        preferred_element_type=F32)
            m_sc[...] = m_new
            return carry

        lax.fori_loop(0, SEQ // ATT_CK, body, 0)

    o = acc_sc[...] / l_sc[...]
    if diff:
        lam = lam_ref[...]
        lam_val = (jnp.exp(jnp.sum(lam[0:1] * lam[1:2], axis=-1, keepdims=True))
                   - jnp.exp(jnp.sum(lam[2:3] * lam[3:4], axis=-1, keepdims=True)) + lam_init)
        d = o[0:tq] - lam_val * o[tq:2 * tq]
        o_ref[...] = (_rms(d, g_ref[...]) * (1.0 - lam_init)).astype(BF16)
    else:
        for g in range(groups):
            o_ref[:, g * HEAD_DIM:(g + 1) * HEAD_DIM] = _rms(o[g * tq:(g + 1) * tq], g_ref[...]).astype(BF16)


def _attn_call(qr, qf, k, v, g_out, lam, *, groups, tq, diff, lam_init, name):
    n_kv = k.shape[0]
    rows = groups * tq
    kv_spec = pl.BlockSpec((1, TOK, HEAD_DIM), lambda h, i: (h, 0, 0))
    q_spec = pl.BlockSpec((groups, tq, HEAD_DIM), lambda h, i: (h, i, 0))
    vec_spec = pl.BlockSpec((1, HEAD_DIM), lambda h, i: (0, 0))
    in_specs = [q_spec, q_spec, kv_spec, kv_spec]
    args = [qr, qf, k, v]
    if diff:
        in_specs.append(pl.BlockSpec((4, B_QK_DIM), lambda h, i: (0, 0)))
        args.append(lam)
        out_w = HEAD_DIM
    else:
        out_w = groups * HEAD_DIM
    in_specs.append(vec_spec)
    args.append(g_out)
    return pl.pallas_call(
        functools.partial(_attn_kernel, groups=groups, tq=tq, diff=diff, lam_init=lam_init),
        grid=(n_kv, TOK // tq),
        in_specs=in_specs,
        out_specs=pl.BlockSpec((tq, out_w), lambda h, i: (i, h)),
        out_shape=jax.ShapeDtypeStruct((TOK, n_kv * out_w), BF16),
        scratch_shapes=[pltpu.VMEM((rows, 1), F32), pltpu.VMEM((rows, 1), F32), pltpu.VMEM((rows, HEAD_DIM), F32)],
        compiler_params=pltpu.CompilerParams(dimension_semantics=("arbitrary", "arbitrary"),
                                             vmem_limit_bytes=VMEM_LIMIT),
        name=name,
    )(*args)


def _pair_max(vals):
    best = None
    for a in range(len(vals)):
        for b in range(a + 1, len(vals)):
            s = vals[a] + vals[b]
            best = s if best is None else jnp.maximum(best, s)
    return best


def _outproj_kernel(ya_ref, yb_ref, yc_ref, w_ref, x_ref, mod_ref, lng_ref, lnb_ref, wr_ref, rb_ref, tri_ref,
                    x1_ref, h2_ref, route_ref, cnt_ref, carry_sc):
    i = pl.program_id(0)

    @pl.when(i == 0)
    def _init():
        carry_sc[...] = jnp.zeros_like(carry_sc)

    y = (jnp.dot(ya_ref[...], w_ref[0:A_Q, :], preferred_element_type=F32)
         + jnp.dot(yb_ref[...], w_ref[A_Q:A_Q + B_V, :], preferred_element_type=F32)
         + jnp.dot(yc_ref[...], w_ref[A_Q + B_V:D_MIX, :], preferred_element_type=F32))
    z = DEEPNORM_ALPHA * x_ref[...] + mod_ref[0, 2:3, :] * y
    x1 = _standardize(z) * lng_ref[...] + lnb_ref[...]
    x1_ref[...] = x1
    h2 = x1 * (1.0 + mod_ref[0, 4:5, :]) + mod_ref[0, 3:4, :]
    h2_ref[...] = h2

    logits = lax.dot_general(wr_ref[...], h2, NT_DIMS, precision=lax.Precision.HIGHEST,
                             preferred_element_type=F32)
    scores = jax.nn.sigmoid(logits)
    biased = scores + rb_ref[...]
    b_rows = [biased[e:e + 1, :] for e in range(N_EXPERTS)]
    s_rows = [scores[e:e + 1, :] for e in range(N_EXPERTS)]
    group_score = [_pair_max(b_rows[g * EXPERTS_PER_GROUP:(g + 1) * EXPERTS_PER_GROUP])
                   for g in range(N_EXPERT_GROUPS)]
    best = group_score[0]
    grp = jnp.zeros_like(best)
    for g in range(1, N_EXPERT_GROUPS):
        better = group_score[g] > best
        grp = jnp.where(better, float(g), grp)
        best = jnp.where(better, group_score[g], best)

    def pick(rows_, j):
        out = rows_[j]
        for g in range(1, N_EXPERT_GROUPS):
            out = jnp.where(grp == float(g), rows_[g * EXPERTS_PER_GROUP + j], out)
        return out

    vb = [pick(b_rows, j) for j in range(EXPERTS_PER_GROUP)]
    vs = [pick(s_rows, j) for j in range(EXPERTS_PER_GROUP)]
    chosen = []
    for j in range(EXPERTS_PER_GROUP):
        rank = jnp.zeros_like(best)
        for k in range(EXPERTS_PER_GROUP):
            if k == j:
                continue
            ahead = (vb[k] > vb[j]) | ((vb[k] == vb[j]) if k < j else False)
            rank = rank + jnp.where(ahead, 1.0, 0.0)
        chosen.append(rank < float(TOP_K))
    loc1 = jnp.full_like(best, float(EXPERTS_PER_GROUP))
    loc2 = jnp.full_like(best, -1.0)
    for j in range(EXPERTS_PER_GROUP):
        loc1 = jnp.where(chosen[j], jnp.minimum(loc1, float(j)), loc1)
        loc2 = jnp.where(chosen[j], jnp.maximum(loc2, float(j)), loc2)
    g1 = jnp.zeros_like(best)
    g2 = jnp.zeros_like(best)
    for j in range(EXPERTS_PER_GROUP):
        g1 = jnp.where(loc1 == float(j), vs[j], g1)
        g2 = jnp.where(loc2 == float(j), vs[j], g2)
    gsum = g1 + g2
    e1 = grp * float(EXPERTS_PER_GROUP) + loc1
    e2 = grp * float(EXPERTS_PER_GROUP) + loc2

    eidx = lax.broadcasted_iota(jnp.int32, (N_EXPERTS, TM), 0).astype(F32)
    is1 = eidx == e1
    is2 = eidx == e2
    sel = jnp.where(is1 | is2, 1.0, 0.0)
    before = jnp.dot(sel.astype(BF16), tri_ref[...], preferred_element_type=F32) + carry_sc[...]
    pos1 = jnp.sum(jnp.where(is1, before, 0.0), axis=0, keepdims=True)
    pos2 = jnp.sum(jnp.where(is2, before, 0.0), axis=0, keepdims=True)
    carry = carry_sc[...] + jnp.sum(sel, axis=1, keepdims=True)
    carry_sc[...] = carry
    cnt_ref[...] = carry[:, :LANES]

    route_ref[0:1, :] = e1
    route_ref[1:2, :] = e2
    route_ref[2:3, :] = pos1
    route_ref[3:4, :] = pos2
    route_ref[4:5, :] = g1 / gsum
    route_ref[5:6, :] = g2 / gsum
    route_ref[6:8, :] = jnp.zeros((2, TM), F32)


def _outproj_call(ya, yb, yc, w_out, xs, mod, ln_g, ln_b, wr_t, rb_rep, tri):
    row_spec = lambda w: pl.BlockSpec((TM, w), lambda i: (i, 0))
    vec_spec = pl.BlockSpec((1, D_MODEL), lambda i: (0, 0))
    return pl.pallas_call(
        _outproj_kernel,
        grid=(N_TILES,),
        in_specs=[
            row_spec(A_Q), row_spec(B_V), row_spec(C_U),
            pl.BlockSpec((D_MIX, D_MODEL), lambda i: (0, 0), pipeline_mode=pl.Buffered(1)),
            row_spec(D_MODEL),
            pl.BlockSpec((1, 6, D_MODEL), lambda i: (jnp.minimum(i, 1), 0, 0)),
            vec_spec, vec_spec,
            pl.BlockSpec((N_EXPERTS, D_MODEL), lambda i: (0, 0)),
            pl.BlockSpec((N_EXPERTS, TM), lambda i: (0, 0)),
            pl.BlockSpec((TM, TM), lambda i: (0, 0)),
        ],
        out_specs=[row_spec(D_MODEL), row_spec(D_MODEL),
                   pl.BlockSpec((8, TM), lambda i: (0, i)),
                   pl.BlockSpec((N_EXPERTS, LANES), lambda i: (0, 0))],
        out_shape=[jax.ShapeDtypeStruct((TOK, D_MODEL), F32), jax.ShapeDtypeStruct((TOK, D_MODEL), F32),
                   jax.ShapeDtypeStruct((8, TOK), F32), jax.ShapeDtypeStruct((N_EXPERTS, LANES), F32)],
        scratch_shapes=[pltpu.VMEM((N_EXPERTS, TM), F32)],
        compiler_params=pltpu.CompilerParams(dimension_semantics=("arbitrary",), vmem_limit_bytes=VMEM_LIMIT),
        name="out_proj",
    )(ya, yb, yc, w_out, xs, mod, ln_g, ln_b, wr_t, rb_rep, tri)


def _row_copy(src_ref, src_row, dst_ref, dst_row, sem):
    return pltpu.make_async_copy(src_ref.at[pl.ds(src_row, 1)], dst_ref.at[pl.ds(dst_row, 1)], sem)


def _dispatch_kernel(dest_ref, h_ref, buf_in_ref, buf_ref, sem):
    del buf_in_ref
    base = pl.program_id(0) * TM

    def start(t, carry):
        tok = base + t
        _row_copy(h_ref, tok, buf_ref, dest_ref[tok], sem).start()
        _row_copy(h_ref, tok, buf_ref, dest_ref[TOK + tok], sem).start()
        return carry

    def wait(t, carry):
        _row_copy(h_ref, 0, buf_ref, 0, sem).wait()
        _row_copy(h_ref, 0, buf_ref, 0, sem).wait()
        return carry

    lax.fori_loop(0, TM, start, 0)
    lax.fori_loop(0, TM, wait, 0)


def _dispatch_call(dest, h2, zeros_buf):
    return pl.pallas_call(
        _dispatch_kernel,
        grid_spec=pltpu.PrefetchScalarGridSpec(
            num_scalar_prefetch=1,
            grid=(N_TILES,),
            in_specs=[pl.BlockSpec(memory_space=pl.ANY), pl.BlockSpec(memory_space=pl.ANY)],
            out_specs=pl.BlockSpec(memory_space=pl.ANY),
            scratch_shapes=[pltpu.SemaphoreType.DMA(())],
        ),
        out_shape=jax.ShapeDtypeStruct((MOE_ROWS, D_MODEL), F32),
        input_output_aliases={2: 0},
        compiler_params=pltpu.CompilerParams(dimension_semantics=("arbitrary",)),
        name="moe_dispatch",
    )(dest, h2, zeros_buf)


def _moe_kernel(be_ref, na_ref, x_ref, wg_ref, wu_ref, wd_ref, o_ref):
    del be_ref
    active = pl.program_id(0) < na_ref[0]

    @pl.when(jnp.logical_not(active))
    def _unused_block():
        o_ref[...] = jnp.zeros_like(o_ref)

    @pl.when(active)
    def _block():
        xb = x_ref[...].astype(BF16)
        gate = jnp.dot(xb, wg_ref[0], preferred_element_type=F32)
        up = jnp.dot(xb, wu_ref[0], preferred_element_type=F32)
        act = (gate * jax.nn.sigmoid(gate) * up).astype(BF16)
        o_ref[...] = jnp.dot(act, wd_ref[0], preferred_element_type=F32)


def _moe_call(block_e, n_active, buf, w_gate, w_up, w_down):
    row_map = lambda b, be, na: (jnp.minimum(b, na[0] - 1), 0)
    w_map = lambda b, be, na: (be[b], 0, 0)
    return pl.pallas_call(
        _moe_kernel,
        grid_spec=pltpu.PrefetchScalarGridSpec(
            num_scalar_prefetch=2,
            grid=(MOE_NB,),
            in_specs=[pl.BlockSpec((MOE_BM, D_MODEL), row_map),
                      pl.BlockSpec((1, D_MODEL, D_EXPERT), w_map),
                      pl.BlockSpec((1, D_MODEL, D_EXPERT), w_map),
                      pl.BlockSpec((1, D_EXPERT, D_MODEL), w_map)],
            out_specs=pl.BlockSpec((MOE_BM, D_MODEL), lambda b, be, na: (b, 0)),
        ),
        out_shape=jax.ShapeDtypeStruct((MOE_ROWS, D_MODEL), F32),
        compiler_params=pltpu.CompilerParams(dimension_semantics=("arbitrary",), vmem_limit_bytes=VMEM_LIMIT),
        name="moe_experts",
    )(block_e, n_active, buf, w_gate, w_up, w_down)


def _combine_kernel(dest_ref, ys_ref, x1_ref, g1_ref, g2_ref, mod_ref, lng_ref, lnb_ref, o_ref, rows_sc, sem):
    base = pl.program_id(0) * TM

    def start(t, carry):
        tok = base + t
        _row_copy(ys_ref, dest_ref[tok], rows_sc.at[0], t, sem).start()
        _row_copy(ys_ref, dest_ref[TOK + tok], rows_sc.at[1], t, sem).start()
        return carry

    def wait(t, carry):
        _row_copy(ys_ref, 0, rows_sc.at[0], 0, sem).wait()
        _row_copy(ys_ref, 0, rows_sc.at[1], 0, sem).wait()
        return carry

    lax.fori_loop(0, TM, start, 0)
    lax.fori_loop(0, TM, wait, 0)
    y = g1_ref[...] * rows_sc[0] + g2_ref[...] * rows_sc[1]
    z = DEEPNORM_ALPHA * x1_ref[...] + mod_ref[0, 5:6, :] * y
    o_ref[...] = _standardize(z) * lng_ref[...] + lnb_ref[...]


def _combine_call(dest, ys, x1, g1, g2, mod, ln_g, ln_b):
    row_spec = lambda w: pl.BlockSpec((TM, w), lambda i, d: (i, 0))
    vec_spec = pl.BlockSpec((1, D_MODEL), lambda i, d: (0, 0))
    return pl.pallas_call(
        _combine_kernel,
        grid_spec=pltpu.PrefetchScalarGridSpec(
            num_scalar_prefetch=1,
            grid=(N_TILES,),
            in_specs=[pl.BlockSpec(memory_space=pl.ANY), row_spec(D_MODEL), row_spec(1), row_spec(1),
                      pl.BlockSpec((1, 6, D_MODEL), lambda i, d: (jnp.minimum(i, 1), 0, 0)),
                      vec_spec, vec_spec],
            out_specs=row_spec(D_MODEL),
            scratch_shapes=[pltpu.VMEM((2, TM, D_MODEL), F32), pltpu.SemaphoreType.DMA(())],
        ),
        out_shape=jax.ShapeDtypeStruct((TOK, D_MODEL), F32),
        compiler_params=pltpu.CompilerParams(dimension_semantics=("arbitrary",), vmem_limit_bytes=VMEM_LIMIT),
        name="moe_combine",
    )(dest, ys, x1, g1, g2, mod, ln_g, ln_b)


def _rope_tables(dim):
    n_freq = dim // 4
    inv = ROPE_THETA ** (-jnp.arange(n_freq, dtype=F32) / n_freq)
    t = jnp.arange(SEQ)
    rows = (t // GRID_W).astype(F32)
    cols = (t % GRID_W).astype(F32)
    lane = jnp.arange(LANES)
    within = lane % dim
    use_col = within >= dim // 2
    freq = within % n_freq
    second_half = (within % (dim // 2)) >= n_freq
    pos = jnp.where(use_col[None, :], cols[:, None], rows[:, None])
    ang = pos * inv[freq][None, :]
    cos = jnp.cos(ang)
    sin = jnp.where(second_half[None, :], jnp.sin(ang), -jnp.sin(ang))
    cos = jnp.concatenate([jnp.ones((CTX, LANES), F32), cos], axis=0)
    sin = jnp.concatenate([jnp.zeros((CTX, LANES), F32), sin], axis=0)
    return cos, sin


def kernel(x, c, ctx, c_ctx, w_ada, b_ada, w_in, w_out, a_q_norm, a_k_norm, a_out_norm, b_lambda, b_out_norm,
           c_spatial, c_spatial_bias, c_out_norm, ln1_g, ln1_b, ln2_g, ln2_b, w_router, router_bias,
           w_gate, w_up, w_down):
    assert x.shape == (1, SEQ, D_MODEL) and ctx.shape == (1, CTX, D_MODEL)
    cos_a, sin_a = _rope_tables(HEAD_DIM)
    cos_b, sin_b = _rope_tables(B_QK_DIM)
    tabs = (cos_a, sin_a, cos_b, sin_b)

    c_rep = jnp.broadcast_to(jnp.stack([c_ctx, c[0]])[:, :, None], (2, D_MODEL, LANES))
    mods = _ada_call(c_rep, w_ada, b_ada).reshape(DEPTH, 2, 6, D_MODEL)

    w_in_b = w_in.astype(BF16)
    w_out_b = w_out.astype(BF16)
    ws_b = c_spatial.astype(BF16)
    bs_rep = jnp.broadcast_to(c_spatial_bias[:, :, :, None], (DEPTH, C_GROUPS, CHUNK, LANES))
    w_gate_b = w_gate.astype(BF16)
    w_up_b = w_up.astype(BF16)
    w_down_b = w_down.astype(BF16)
    wr_t = w_router.T
    rb_rep = jnp.broadcast_to(router_bias[:, None], (N_EXPERTS, TM))
    tri = jnp.triu(jnp.ones((TM, TM), BF16), k=1)

    xs = jnp.concatenate([ctx[0], x[0]], axis=0)
    for l in range(DEPTH):
        lam_init = 0.8 - 0.6 * math.exp(-0.3 * l)
        mod = mods[l]
        vec = lambda a: a[l].reshape(1, -1)
        qar, qaf, ka, va, qbr, qbf, kb, vb, yc = _inproj_call(
            xs, mod, w_in_b[l], tabs, vec(a_q_norm), vec(a_k_norm), ws_b[l], bs_rep[l], vec(c_out_norm))
        ya = _attn_call(qar, qaf, ka, va, vec(a_out_norm), None, groups=A_GROUP, tq=ATT_TQ_A, diff=False,
                        lam_init=lam_init, name="attn_gqa")
        yb = _attn_call(qbr, qbf, kb, vb, vec(b_out_norm), b_lambda[l], groups=2, tq=ATT_TQ_B, diff=True,
                        lam_init=lam_init, name="attn_diff")
        x1, h2, route, counts = _outproj_call(ya, yb, yc, w_out_b[l], xs, mod, vec(ln1_g), vec(ln1_b),
                                              wr_t, rb_rep, tri)

        cnt = counts[:, 0].astype(jnp.int32)
        padded = (cnt + MOE_BM - 1) // MOE_BM * MOE_BM
        ends = jnp.cumsum(padded)
        starts = ends - padded
        e1 = route[0].astype(jnp.int32)
        e2 = route[1].astype(jnp.int32)
        dest = jnp.concatenate([starts[e1] + route[2].astype(jnp.int32), starts[e2] + route[3].astype(jnp.int32)])
        n_active = (ends[-1] // MOE_BM).astype(jnp.int32)
        blk = jnp.minimum(jnp.arange(MOE_NB, dtype=jnp.int32), n_active - 1)
        block_e = jnp.minimum(jnp.searchsorted(ends, blk * MOE_BM, side="right"), N_EXPERTS - 1).astype(jnp.int32)

        buf = _dispatch_call(dest, h2, jnp.zeros((MOE_ROWS, D_MODEL), F32))
        ys = _moe_call(block_e, n_active.reshape(1), buf, w_gate_b[l], w_up_b[l], w_down_b[l])
        xs = _combine_call(dest, ys, x1, route[4].reshape(TOK, 1), route[5].reshape(TOK, 1), mod,
                           vec(ln2_g), vec(ln2_b))
    return xs[CTX:].reshape(1, SEQ, D_MODEL)
```

```python
import functools
import math

import jax
import jax.numpy as jnp
from jax import lax
from jax.experimental import pallas as pl
from jax.experimental.pallas import tpu as pltpu

F32 = jnp.float32
BF16 = jnp.bfloat16

D_MODEL = 2048
SEQ = 8192
CTX = 256
TOK = CTX + SEQ
DEPTH = 4
GRID_W = 64

HEAD_DIM = 128
A_HEADS = 8
A_KV_HEADS = 2
A_GROUP = A_HEADS // A_KV_HEADS
B_HEADS = 4
B_QK_DIM = 64
C_GROUPS = 4
CHUNK = 128
ROPE_THETA = 10000.0

A_Q = A_HEADS * HEAD_DIM
A_KV = A_KV_HEADS * HEAD_DIM
B_QK = B_HEADS * HEAD_DIM
B_V = B_HEADS * HEAD_DIM
C_U = C_GROUPS * HEAD_DIM
D_IN = A_Q + 2 * A_KV + 2 * B_QK + B_V + 2 * C_U
D_MIX = A_Q + B_V + C_U
OFF_AQ = 0
OFF_AK = OFF_AQ + A_Q
OFF_AV = OFF_AK + A_KV
OFF_BQ = OFF_AV + A_KV
OFF_BK = OFF_BQ + B_QK
OFF_BV = OFF_BK + B_QK
OFF_CU = OFF_BV + B_V
OFF_CV = OFF_CU + C_U

N_EXPERTS = 16
N_EXPERT_GROUPS = 4
EXPERTS_PER_GROUP = 4
TOP_K = 2
D_EXPERT = 1024

DEEPNORM_ALPHA = (2 * DEPTH) ** 0.25
NORM_EPS = 1e-6

LANES = 128
VMEM_LIMIT = 60 * 1024 * 1024

TM = 256
N_TILES = TOK // TM
ADA_TN = 512
ATT_CK = 512
ATT_TQ_A = 128
ATT_TQ_B = 256
MOE_BM = 256
MOE_NB = (TOK * TOP_K + N_EXPERTS * (MOE_BM - 1) + MOE_BM - 1) // MOE_BM
MOE_ROWS = MOE_NB * MOE_BM

NT_DIMS = (((1,), (1,)), ((), ()))


def _rms(x, g):
    return x * lax.rsqrt(jnp.mean(x * x, axis=-1, keepdims=True) + NORM_EPS) * g


def _gelu(x):
    return 0.5 * x * (1.0 + lax.erf(x * (2.0 ** -0.5)))


def _standardize(x):
    mu = jnp.mean(x, axis=-1, keepdims=True)
    xc = x - mu
    var = jnp.mean(xc * xc, axis=-1, keepdims=True)
    return xc * lax.rsqrt(var + NORM_EPS)


def _ada_kernel(c_ref, w_ref, b_ref, o_ref):
    for r in range(2):
        cv = c_ref[r]
        act = cv * jax.nn.sigmoid(cv)
        for j in range(ADA_TN // LANES):
            cols = slice(j * LANES, (j + 1) * LANES)
            o_ref[0, r:r + 1, cols] = jnp.sum(w_ref[0, :, cols] * act, axis=0, keepdims=True) + b_ref[0, :, cols]


def _ada_call(c_rep, w_ada, b_ada):
    n_out = w_ada.shape[-1]
    return pl.pallas_call(
        _ada_kernel,
        grid=(DEPTH, n_out // ADA_TN),
        in_specs=[
            pl.BlockSpec((2, D_MODEL, LANES), lambda l, n: (0, 0, 0)),
            pl.BlockSpec((1, D_MODEL, ADA_TN), lambda l, n: (l, 0, n)),
            pl.BlockSpec((1, 1, ADA_TN), lambda l, n: (l, 0, n)),
        ],
        out_specs=pl.BlockSpec((1, 2, ADA_TN), lambda l, n: (l, 0, n)),
        out_shape=jax.ShapeDtypeStruct((DEPTH, 2, n_out), F32),
        compiler_params=pltpu.CompilerParams(dimension_semantics=("arbitrary", "arbitrary"),
                                             vmem_limit_bytes=VMEM_LIMIT),
        name="ada_ln",
    )(c_rep, w_ada, b_ada.reshape(DEPTH, 1, n_out))


def _rope(x, cos, sin_signed, half, lane):
    fwd = pltpu.roll(x, LANES - half, 1)
    bwd = pltpu.roll(x, half, 1)
    partner = jnp.where((lane & (2 * half - 1)) < half, fwd, bwd)
    return x * cos + partner * sin_signed


def _inproj_kernel(x_ref, mod_ref, w_ref, cosa_ref, sina_ref, cosb_ref, sinb_ref, gq_ref, gk_ref,
                   ws_ref, bs_ref, gc_ref,
                   qar_ref, qaf_ref, ka_ref, va_ref, qbr_ref, qbf_ref, kb_ref, vb_ref, yc_ref):
    x = x_ref[...]
    h = (x * (1.0 + mod_ref[0, 1:2, :]) + mod_ref[0, 0:1, :]).astype(BF16)
    lane = lax.broadcasted_iota(jnp.int32, (TM, LANES), 1)
    cosa, sina = cosa_ref[...], sina_ref[...]
    cosb, sinb = cosb_ref[...], sinb_ref[...]
    scale_a = HEAD_DIM ** -0.5
    scale_b = B_QK_DIM ** -0.5

    def proj(col):
        p = jnp.dot(h, w_ref[:, col:col + 2 * LANES], preferred_element_type=F32)
        return p[:, :LANES], p[:, LANES:]

    for j2 in range(A_HEADS // 2):
        for j, q in zip((2 * j2, 2 * j2 + 1), proj(OFF_AQ + j2 * 2 * LANES)):
            qn = _rms(q, gq_ref[...]) * scale_a
            qaf_ref[j] = qn.astype(BF16)
            qar_ref[j] = _rope(qn, cosa, sina, 32, lane).astype(BF16)
    for j, k in enumerate(proj(OFF_AK)):
        ka_ref[j] = _rope(_rms(k, gk_ref[...]), cosa, sina, 32, lane).astype(BF16)
    for j, v in enumerate(proj(OFF_AV)):
        va_ref[j] = v.astype(BF16)
    for j2 in range(B_HEADS // 2):
        for j, q in zip((2 * j2, 2 * j2 + 1), proj(OFF_BQ + j2 * 2 * LANES)):
            qs = q * scale_b
            qrot = _rope(qs, cosb, sinb, 16, lane)
            first = lane < B_QK_DIM
            qbf_ref[2 * j] = jnp.where(first, qs, 0.0).astype(BF16)
            qbf_ref[2 * j + 1] = jnp.where(first, 0.0, qs).astype(BF16)
            qbr_ref[2 * j] = jnp.where(first, qrot, 0.0).astype(BF16)
            qbr_ref[2 * j + 1] = jnp.where(first, 0.0, qrot).astype(BF16)
    for j2 in range(B_HEADS // 2):
        for j, k in zip((2 * j2, 2 * j2 + 1), proj(OFF_BK + j2 * 2 * LANES)):
            kb_ref[j] = _rope(k, cosb, sinb, 16, lane).astype(BF16)
    for j2 in range(B_HEADS // 2):
        for j, v in zip((2 * j2, 2 * j2 + 1), proj(OFF_BV + j2 * 2 * LANES)):
            vb_ref[j] = v.astype(BF16)
    for j2 in range(C_GROUPS // 2):
        us = proj(OFF_CU + j2 * 2 * LANES)
        vs = proj(OFF_CV + j2 * 2 * LANES)
        for g, u, v in zip((2 * j2, 2 * j2 + 1), us, vs):
            u = _gelu(u)
            v = _standardize(_gelu(v)).astype(BF16)
            for c in range(TM // CHUNK):
                rows = slice(c * CHUNK, (c + 1) * CHUNK)
                mixed = jnp.dot(ws_ref[g], v[rows], preferred_element_type=F32) + bs_ref[g]
                yc_ref[rows, g * LANES:(g + 1) * LANES] = _rms(u[rows] * mixed, gc_ref[...]).astype(BF16)


def _inproj_call(xs, mod, w_in, tabs, gq, gk, ws, bs, gc):
    def heads(n):
        return (jax.ShapeDtypeStruct((n, TOK, HEAD_DIM), BF16),
                pl.BlockSpec((n, TM, HEAD_DIM), lambda i: (0, i, 0)))

    outs = [heads(A_HEADS), heads(A_HEADS), heads(A_KV_HEADS), heads(A_KV_HEADS),
            heads(2 * B_HEADS), heads(2 * B_HEADS), heads(B_HEADS), heads(B_HEADS),
            (jax.ShapeDtypeStruct((TOK, C_U), BF16), pl.BlockSpec((TM, C_U), lambda i: (i, 0)))]
    tab_spec = pl.BlockSpec((TM, LANES), lambda i: (i, 0))
    vec_spec = pl.BlockSpec((1, LANES), lambda i: (0, 0))
    return pl.pallas_call(
        _inproj_kernel,
        grid=(N_TILES,),
        in_specs=[
            pl.BlockSpec((TM, D_MODEL), lambda i: (i, 0)),
            pl.BlockSpec((1, 6, D_MODEL), lambda i: (jnp.minimum(i, 1), 0, 0)),
            pl.BlockSpec((D_MODEL, D_IN), lambda i: (0, 0), pipeline_mode=pl.Buffered(1)),
            tab_spec, tab_spec, tab_spec, tab_spec, vec_spec, vec_spec,
            pl.BlockSpec((C_GROUPS, CHUNK, CHUNK), lambda i: (0, 0, 0)),
            pl.BlockSpec((C_GROUPS, CHUNK, LANES), lambda i: (0, 0, 0)),
            vec_spec,
        ],
        out_specs=[o[1] for o in outs],
        out_shape=[o[0] for o in outs],
        compiler_params=pltpu.CompilerParams(dimension_semantics=("arbitrary",), vmem_limit_bytes=VMEM_LIMIT),
        name="in_proj",
    )(xs, mod, w_in, *tabs, gq, gk, ws, bs, gc)


def _attn_kernel(*refs, groups, tq, diff, lam_init):
    if diff:
        qr_ref, qf_ref, k_ref, v_ref, lam_ref, g_ref, o_ref, m_sc, l_sc, acc_sc = refs
    else:
        qr_ref, qf_ref, k_ref, v_ref, g_ref, o_ref, m_sc, l_sc, acc_sc = refs
    i = pl.program_id(1)
    rows = groups * tq

    qf = qf_ref[...].reshape(rows, HEAD_DIM)
    s = lax.dot_general(qf, k_ref[0, 0:CTX, :], NT_DIMS, preferred_element_type=F32)
    m0 = jnp.max(s, axis=-1, keepdims=True)
    p = jnp.exp(s - m0)
    m_sc[...] = m0
    l_sc[...] = jnp.sum(p, axis=-1, keepdims=True)
    acc_sc[...] = jnp.dot(p.astype(BF16), v_ref[0, 0:CTX, :], preferred_element_type=F32)

    @pl.when(i >= CTX // tq)
    def _latent_keys():
        qr = qr_ref[...].reshape(rows, HEAD_DIM)

        def body(j, carry):
            off = pl.multiple_of(CTX + j * ATT_CK, CTX)
            s = lax.dot_general(qr, k_ref[0, pl.ds(off, ATT_CK), :], NT_DIMS, preferred_element_type=F32)
            m_prev = m_sc[...]
            m_new = jnp.maximum(m_prev, jnp.max(s, axis=-1, keepdims=True))
            alpha = jnp.exp(m_prev - m_new)
            p = jnp.exp(s - m_new)
            l_sc[...] = alpha * l_sc[...] + jnp.sum(p, axis=-1, keepdims=True)
            acc_sc[...] = alpha * acc_sc[...] + jnp.dot(p.astype(BF16), v_ref[0, pl.ds(off, ATT_CK), :],
                                                        preferred_element_type=F32)
            m_sc[...] = m_new
            return carry

        lax.fori_loop(0, SEQ // ATT_CK, body, 0)

    o = acc_sc[...] / l_sc[...]
    if diff:
        lam = lam_ref[...]
        lam_val = (jnp.exp(jnp.sum(lam[0:1] * lam[1:2], axis=-1, keepdims=True))
                   - jnp.exp(jnp.sum(lam[2:3] * lam[3:4], axis=-1, keepdims=True)) + lam_init)
        d = o[0:tq] - lam_val * o[tq:2 * tq]
        o_ref[...] = (_rms(d, g_ref[...]) * (1.0 - lam_init)).astype(BF16)
    else:
        for g in range(groups):
            o_ref[:, g * HEAD_DIM:(g + 1) * HEAD_DIM] = _rms(o[g * tq:(g + 1) * tq], g_ref[...]).astype(BF16)


def _attn_call(qr, qf, k, v, g_out, lam, *, groups, tq, diff, lam_init, name):
    n_kv = k.shape[0]
    rows = groups * tq
    kv_spec = pl.BlockSpec((1, TOK, HEAD_DIM), lambda h, i: (h, 0, 0))
    q_spec = pl.BlockSpec((groups, tq, HEAD_DIM), lambda h, i: (h, i, 0))
    vec_spec = pl.BlockSpec((1, HEAD_DIM), lambda h, i: (0, 0))
    in_specs = [q_spec, q_spec, kv_spec, kv_spec]
    args = [qr, qf, k, v]
    if diff:
        in_specs.append(pl.BlockSpec((4, B_QK_DIM), lambda h, i: (0, 0)))
        args.append(lam)
        out_w = HEAD_DIM
    else:
        out_w = groups * HEAD_DIM
    in_specs.append(vec_spec)
    args.append(g_out)
    return pl.pallas_call(
        functools.partial(_attn_kernel, groups=groups, tq=tq, diff=diff, lam_init=lam_init),
        grid=(n_kv, TOK // tq),
        in_specs=in_specs,
        out_specs=pl.BlockSpec((tq, out_w), lambda h, i: (i, h)),
        out_shape=jax.ShapeDtypeStruct((TOK, n_kv * out_w), BF16),
        scratch_shapes=[pltpu.VMEM((rows, 1), F32), pltpu.VMEM((rows, 1), F32), pltpu.VMEM((rows, HEAD_DIM), F32)],
        compiler_params=pltpu.CompilerParams(dimension_semantics=("arbitrary", "arbitrary"),
                                             vmem_limit_bytes=VMEM_LIMIT),
        name=name,
    )(*args)


def _pair_max(vals):
    best = None
    for a in range(len(vals)):
        for b in range(a + 1, len(vals)):
            s = vals[a] + vals[b]
            best = s if best is None else jnp.maximum(best, s)
    return best


def _outproj_kernel(ya_ref, yb_ref, yc_ref, w_ref, x_ref, mod_ref, lng_ref, lnb_ref, wr_ref, rb_ref, tri_ref,
                    x1_ref, h2_ref, route_ref, cnt_ref, carry_sc):
    i = pl.program_id(0)

    @pl.when(i == 0)
    def _init():
        carry_sc[...] = jnp.zeros_like(carry_sc)

    y = (jnp.dot(ya_ref[...], w_ref[0:A_Q, :], preferred_element_type=F32)
         + jnp.dot(yb_ref[...], w_ref[A_Q:A_Q + B_V, :], preferred_element_type=F32)
         + jnp.dot(yc_ref[...], w_ref[A_Q + B_V:D_MIX, :], preferred_element_type=F32))
    z = DEEPNORM_ALPHA * x_ref[...] + mod_ref[0, 2:3, :] * y
    x1 = _standardize(z) * lng_ref[...] + lnb_ref[...]
    x1_ref[...] = x1
    h2 = x1 * (1.0 + mod_ref[0, 4:5, :]) + mod_ref[0, 3:4, :]
    h2_ref[...] = h2

    logits = lax.dot_general(wr_ref[...], h2, NT_DIMS, precision=lax.Precision.HIGHEST,
                             preferred_element_type=F32)
    scores = jax.nn.sigmoid(logits)
    biased = scores + rb_ref[...]
    b_rows = [biased[e:e + 1, :] for e in range(N_EXPERTS)]
    s_rows = [scores[e:e + 1, :] for e in range(N_EXPERTS)]
    group_score = [_pair_max(b_rows[g * EXPERTS_PER_GROUP:(g + 1) * EXPERTS_PER_GROUP])
                   for g in range(N_EXPERT_GROUPS)]
    best = group_score[0]
    grp = jnp.zeros_like(best)
    for g in range(1, N_EXPERT_GROUPS):
        better = group_score[g] > best
        grp = jnp.where(better, float(g), grp)
        best = jnp.where(better, group_score[g], best)

    def pick(rows_, j):
        out = rows_[j]
        for g in range(1, N_EXPERT_GROUPS):
            out = jnp.where(grp == float(g), rows_[g * EXPERTS_PER_GROUP + j], out)
        return out

    vb = [pick(b_rows, j) for j in range(EXPERTS_PER_GROUP)]
    vs = [pick(s_rows, j) for j in range(EXPERTS_PER_GROUP)]
    chosen = []
    for j in range(EXPERTS_PER_GROUP):
        rank = jnp.zeros_like(best)
        for k in range(EXPERTS_PER_GROUP):
            if k == j:
                continue
            ahead = (vb[k] > vb[j]) | ((vb[k] == vb[j]) if k < j else False)
            rank = rank + jnp.where(ahead, 1.0, 0.0)
        chosen.append(rank < float(TOP_K))
    loc1 = jnp.full_like(best, float(EXPERTS_PER_GROUP))
    loc2 = jnp.full_like(best, -1.0)
    for j in range(EXPERTS_PER_GROUP):
        loc1 = jnp.where(chosen[j], jnp.minimum(loc1, float(j)), loc1)
        loc2 = jnp.where(chosen[j], jnp.maximum(loc2, float(j)), loc2)
    g1 = jnp.zeros_like(best)
    g2 = jnp.zeros_like(best)
    for j in range(EXPERTS_PER_GROUP):
        g1 = jnp.where(loc1 == float(j), vs[j], g1)
        g2 = jnp.where(loc2 == float(j), vs[j], g2)
    gsum = g1 + g2
    e1 = grp * float(EXPERTS_PER_GROUP) + loc1
    e2 = grp * float(EXPERTS_PER_GROUP) + loc2

    eidx = lax.broadcasted_iota(jnp.int32, (N_EXPERTS, TM), 0).astype(F32)
    is1 = eidx == e1
    is2 = eidx == e2
    sel = jnp.where(is1 | is2, 1.0, 0.0)
    before = jnp.dot(sel.astype(BF16), tri_ref[...], preferred_element_type=F32) + carry_sc[...]
    pos1 = jnp.sum(jnp.where(is1, before, 0.0), axis=0, keepdims=True)
    pos2 = jnp.sum(jnp.where(is2, before, 0.0), axis=0, keepdims=True)
    carry = carry_sc[...] + jnp.sum(sel, axis=1, keepdims=True)
    carry_sc[...] = carry
    cnt_ref[...] = carry[:, :LANES]

    route_ref[0:1, :] = e1
    route_ref[1:2, :] = e2
    route_ref[2:3, :] = pos1
    route_ref[3:4, :] = pos2
    route_ref[4:5, :] = g1 / gsum
    route_ref[5:6, :] = g2 / gsum
    route_ref[6:8, :] = jnp.zeros((2, TM), F32)


def _outproj_call(ya, yb, yc, w_out, xs, mod, ln_g, ln_b, wr_t, rb_rep, tri):
    row_spec = lambda w: pl.BlockSpec((TM, w), lambda i: (i, 0))
    vec_spec = pl.BlockSpec((1, D_MODEL), lambda i: (0, 0))
    return pl.pallas_call(
        _outproj_kernel,
        grid=(N_TILES,),
        in_specs=[
            row_spec(A_Q), row_spec(B_V), row_spec(C_U),
            pl.BlockSpec((D_MIX, D_MODEL), lambda i: (0, 0), pipeline_mode=pl.Buffered(1)),
            row_spec(D_MODEL),
            pl.BlockSpec((1, 6, D_MODEL), lambda i: (jnp.minimum(i, 1), 0, 0)),
            vec_spec, vec_spec,
            pl.BlockSpec((N_EXPERTS, D_MODEL), lambda i: (0, 0)),
            pl.BlockSpec((N_EXPERTS, TM), lambda i: (0, 0)),
            pl.BlockSpec((TM, TM), lambda i: (0, 0)),
        ],
        out_specs=[row_spec(D_MODEL), row_spec(D_MODEL),
                   pl.BlockSpec((8, TM), lambda i: (0, i)),
                   pl.BlockSpec((N_EXPERTS, LANES), lambda i: (0, 0))],
        out_shape=[jax.ShapeDtypeStruct((TOK, D_MODEL), F32), jax.ShapeDtypeStruct((TOK, D_MODEL), F32),
                   jax.ShapeDtypeStruct((8, TOK), F32), jax.ShapeDtypeStruct((N_EXPERTS, LANES), F32)],
        scratch_shapes=[pltpu.VMEM((N_EXPERTS, TM), F32)],
        compiler_params=pltpu.CompilerParams(dimension_semantics=("arbitrary",), vmem_limit_bytes=VMEM_LIMIT),
        name="out_proj",
    )(ya, yb, yc, w_out, xs, mod, ln_g, ln_b, wr_t, rb_rep, tri)


def _row_copy(src_ref, src_row, dst_ref, dst_row, sem):
    return pltpu.make_async_copy(src_ref.at[pl.ds(src_row, 1)], dst_ref.at[pl.ds(dst_row, 1)], sem)


def _slot_source_kernel(dest_ref, src_ref):
    def clear(s, carry):
        src_ref[s] = 0
        return carry

    def scatter(t, carry):
        src_ref[dest_ref[t]] = t
        src_ref[dest_ref[TOK + t]] = t
        return carry

    lax.fori_loop(0, MOE_ROWS, clear, 0, unroll=8)
    lax.fori_loop(0, TOK, scatter, 0, unroll=8)


def _slot_source_call(dest):
    return pl.pallas_call(
        _slot_source_kernel,
        in_specs=[pl.BlockSpec(memory_space=pltpu.SMEM)],
        out_specs=pl.BlockSpec(memory_space=pltpu.SMEM),
        out_shape=jax.ShapeDtypeStruct((MOE_ROWS,), jnp.int32),
        name="moe_slot_source",
    )(dest)


def _moe_kernel(be_ref, na_ref, src_ref, h_ref, wg_ref, wu_ref, wd_ref, o_ref, x_sc, sems):
    del be_ref
    b = pl.program_id(0)
    n_active = na_ref[0]

    def start_gather(block, slot):
        def body(r, carry):
            _row_copy(h_ref, src_ref[block * MOE_BM + r], x_sc.at[slot], r, sems.at[slot]).start()
            return carry
        lax.fori_loop(0, MOE_BM, body, 0, unroll=8)

    def wait_gather(slot):
        def body(r, carry):
            _row_copy(h_ref, 0, x_sc.at[slot], 0, sems.at[slot]).wait()
            return carry
        lax.fori_loop(0, MOE_BM, body, 0, unroll=8)

    @pl.when(b == 0)
    def _first():
        start_gather(0, 0)

    @pl.when(b + 1 < n_active)
    def _prefetch_next():
        start_gather(b + 1, (b + 1) % 2)

    @pl.when(b >= n_active)
    def _unused_block():
        o_ref[...] = jnp.zeros_like(o_ref)

    @pl.when(b < n_active)
    def _block():
        slot = b % 2
        wait_gather(slot)
        xb = x_sc[slot].astype(BF16)
        gate = jnp.dot(xb, wg_ref[0], preferred_element_type=F32)
        up = jnp.dot(xb, wu_ref[0], preferred_element_type=F32)
        act = (gate * jax.nn.sigmoid(gate) * up).astype(BF16)
        o_ref[...] = jnp.dot(act, wd_ref[0], preferred_element_type=F32)


def _moe_call(block_e, n_active, src, h2, w_gate, w_up, w_down):
    w_map = lambda b, be, na, sr: (be[b], 0, 0)
    return pl.pallas_call(
        _moe_kernel,
        grid_spec=pltpu.PrefetchScalarGridSpec(
            num_scalar_prefetch=3,
            grid=(MOE_NB,),
            in_specs=[pl.BlockSpec(memory_space=pl.ANY),
                      pl.BlockSpec((1, D_MODEL, D_EXPERT), w_map),
                      pl.BlockSpec((1, D_MODEL, D_EXPERT), w_map),
                      pl.BlockSpec((1, D_EXPERT, D_MODEL), w_map)],
            out_specs=pl.BlockSpec((MOE_BM, D_MODEL), lambda b, be, na, sr: (b, 0)),
            scratch_shapes=[pltpu.VMEM((2, MOE_BM, D_MODEL), F32), pltpu.SemaphoreType.DMA((2,))],
        ),
        out_shape=jax.ShapeDtypeStruct((MOE_ROWS, D_MODEL), F32),
        compiler_params=pltpu.CompilerParams(dimension_semantics=("arbitrary",), vmem_limit_bytes=VMEM_LIMIT),
        name="moe_experts",
    )(block_e, n_active, src, h2, w_gate, w_up, w_down)


def _combine_kernel(dest_ref, ys_ref, x1_ref, g1_ref, g2_ref, mod_ref, lng_ref, lnb_ref, o_ref, rows_sc, sem):
    base = pl.program_id(0) * TM

    def start(t, carry):
        tok = base + t
        _row_copy(ys_ref, dest_ref[tok], rows_sc.at[0], t, sem).start()
        _row_copy(ys_ref, dest_ref[TOK + tok], rows_sc.at[1], t, sem).start()
        return carry

    def wait(t, carry):
        _row_copy(ys_ref, 0, rows_sc.at[0], 0, sem).wait()
        _row_copy(ys_ref, 0, rows_sc.at[1], 0, sem).wait()
        return carry

    lax.fori_loop(0, TM, start, 0)
    lax.fori_loop(0, TM, wait, 0)
    y = g1_ref[...] * rows_sc[0] + g2_ref[...] * rows_sc[1]
    z = DEEPNORM_ALPHA * x1_ref[...] + mod_ref[0, 5:6, :] * y
    o_ref[...] = _standardize(z) * lng_ref[...] + lnb_ref[...]


def _combine_call(dest, ys, x1, g1, g2, mod, ln_g, ln_b):
    row_spec = lambda w: pl.BlockSpec((TM, w), lambda i, d: (i, 0))
    vec_spec = pl.BlockSpec((1, D_MODEL), lambda i, d: (0, 0))
    return pl.pallas_call(
        _combine_kernel,
        grid_spec=pltpu.PrefetchScalarGridSpec(
            num_scalar_prefetch=1,
            grid=(N_TILES,),
            in_specs=[pl.BlockSpec(memory_space=pl.ANY), row_spec(D_MODEL), row_spec(1), row_spec(1),
                      pl.BlockSpec((1, 6, D_MODEL), lambda i, d: (jnp.minimum(i, 1), 0, 0)),
                      vec_spec, vec_spec],
            out_specs=row_spec(D_MODEL),
            scratch_shapes=[pltpu.VMEM((2, TM, D_MODEL), F32), pltpu.SemaphoreType.DMA(())],
        ),
        out_shape=jax.ShapeDtypeStruct((TOK, D_MODEL), F32),
        compiler_params=pltpu.CompilerParams(dimension_semantics=("arbitrary",), vmem_limit_bytes=VMEM_LIMIT),
        name="moe_combine",
    )(dest, ys, x1, g1, g2, mod, ln_g, ln_b)


def _rope_tables(dim):
    n_freq = dim // 4
    inv = ROPE_THETA ** (-jnp.arange(n_freq, dtype=F32) / n_freq)
    t = jnp.arange(SEQ)
    rows = (t // GRID_W).astype(F32)
    cols = (t % GRID_W).astype(F32)
    lane = jnp.arange(LANES)
    within = lane % dim
    use_col = within >= dim // 2
    freq = within % n_freq
    second_half = (within % (dim // 2)) >= n_freq
    pos = jnp.where(use_col[None, :], cols[:, None], rows[:, None])
    ang = pos * inv[freq][None, :]
    cos = jnp.cos(ang)
    sin = jnp.where(second_half[None, :], jnp.sin(ang), -jnp.sin(ang))
    cos = jnp.concatenate([jnp.ones((CTX, LANES), F32), cos], axis=0)
    sin = jnp.concatenate([jnp.zeros((CTX, LANES), F32), sin], axis=0)
    return cos, sin


def kernel(x, c, ctx, c_ctx, w_ada, b_ada, w_in, w_out, a_q_norm, a_k_norm, a_out_norm, b_lambda, b_out_norm,
           c_spatial, c_spatial_bias, c_out_norm, ln1_g, ln1_b, ln2_g, ln2_b, w_router, router_bias,
           w_gate, w_up, w_down):
    assert x.shape == (1, SEQ, D_MODEL) and ctx.shape == (1, CTX, D_MODEL)
    cos_a, sin_a = _rope_tables(HEAD_DIM)
    cos_b, sin_b = _rope_tables(B_QK_DIM)
    tabs = (cos_a, sin_a, cos_b, sin_b)

    c_rep = jnp.broadcast_to(jnp.stack([c_ctx, c[0]])[:, :, None], (2, D_MODEL, LANES))
    mods = _ada_call(c_rep, w_ada, b_ada).reshape(DEPTH, 2, 6, D_MODEL)

    w_in_b = w_in.astype(BF16)
    w_out_b = w_out.astype(BF16)
    ws_b = c_spatial.astype(BF16)
    bs_rep = jnp.broadcast_to(c_spatial_bias[:, :, :, None], (DEPTH, C_GROUPS, CHUNK, LANES))
    w_gate_b = w_gate.astype(BF16)
    w_up_b = w_up.astype(BF16)
    w_down_b = w_down.astype(BF16)
    wr_t = w_router.T
    rb_rep = jnp.broadcast_to(router_bias[:, None], (N_EXPERTS, TM))
    tri = jnp.triu(jnp.ones((TM, TM), BF16), k=1)

    xs = jnp.concatenate([ctx[0], x[0]], axis=0)
    for l in range(DEPTH):
        lam_init = 0.8 - 0.6 * math.exp(-0.3 * l)
        mod = mods[l]
        vec = lambda a: a[l].reshape(1, -1)
        qar, qaf, ka, va, qbr, qbf, kb, vb, yc = _inproj_call(
            xs, mod, w_in_b[l], tabs, vec(a_q_norm), vec(a_k_norm), ws_b[l], bs_rep[l], vec(c_out_norm))
        ya = _attn_call(qar, qaf, ka, va, vec(a_out_norm), None, groups=A_GROUP, tq=ATT_TQ_A, diff=False,
                        lam_init=lam_init, name="attn_gqa")
        yb = _attn_call(qbr, qbf, kb, vb, vec(b_out_norm), b_lambda[l], groups=2, tq=ATT_TQ_B, diff=True,
                        lam_init=lam_init, name="attn_diff")
        x1, h2, route, counts = _outproj_call(ya, yb, yc, w_out_b[l], xs, mod, vec(ln1_g), vec(ln1_b),
                                              wr_t, rb_rep, tri)

        cnt = counts[:, 0].astype(jnp.int32)
        padded = (cnt + MOE_BM - 1) // MOE_BM * MOE_BM
        ends = jnp.cumsum(padded)
        starts = ends - padded
        e1 = route[0].astype(jnp.int32)
        e2 = route[1].astype(jnp.int32)
        dest = jnp.concatenate([starts[e1] + route[2].astype(jnp.int32), starts[e2] + route[3].astype(jnp.int32)])
        n_active = (ends[-1] // MOE_BM).astype(jnp.int32)
        blk = jnp.minimum(jnp.arange(MOE_NB, dtype=jnp.int32), n_active - 1)
        block_e = jnp.minimum(jnp.searchsorted(ends, blk * MOE_BM, side="right"), N_EXPERTS - 1).astype(jnp.int32)

        src = _slot_source_call(dest)
        ys = _moe_call(block_e, n_active.reshape(1), src, h2, w_gate_b[l], w_up_b[l], w_down_b[l])
        xs = _combine_call(dest, ys, x1, route[4].reshape(TOK, 1), route[5].reshape(TOK, 1), mod,
                           vec(ln2_g), vec(ln2_b))
    return xs[CTX:].reshape(1, SEQ, D_MODEL)
```

```python
import functools
import math

import jax
import jax.numpy as jnp
from jax import lax
from jax.experimental import pallas as pl
from jax.experimental.pallas import tpu as pltpu

F32 = jnp.float32
BF16 = jnp.bfloat16

D_MODEL = 2048
SEQ = 8192
CTX = 256
TOK = CTX + SEQ
DEPTH = 4
GRID_W = 64

HEAD_DIM = 128
A_HEADS = 8
A_KV_HEADS = 2
A_GROUP = A_HEADS // A_KV_HEADS
B_HEADS = 4
B_QK_DIM = 64
C_GROUPS = 4
CHUNK = 128
ROPE_THETA = 10000.0

A_Q = A_HEADS * HEAD_DIM
A_KV = A_KV_HEADS * HEAD_DIM
B_QK = B_HEADS * HEAD_DIM
B_V = B_HEADS * HEAD_DIM
C_U = C_GROUPS * HEAD_DIM
D_IN = A_Q + 2 * A_KV + 2 * B_QK + B_V + 2 * C_U
D_MIX = A_Q + B_V + C_U
OFF_AQ = 0
OFF_AK = OFF_AQ + A_Q
OFF_AV = OFF_AK + A_KV
OFF_BQ = OFF_AV + A_KV
OFF_BK = OFF_BQ + B_QK
OFF_BV = OFF_BK + B_QK
OFF_CU = OFF_BV + B_V
OFF_CV = OFF_CU + C_U

N_EXPERTS = 16
N_EXPERT_GROUPS = 4
EXPERTS_PER_GROUP = 4
TOP_K = 2
D_EXPERT = 1024

DEEPNORM_ALPHA = (2 * DEPTH) ** 0.25
NORM_EPS = 1e-6
LOG2_E = math.log2(math.e)

LANES = 128
VMEM_LIMIT = 60 * 1024 * 1024

TM = 256
N_TILES = TOK // TM
ADA_TN = 512
ATT_CK = 512
ATT_TQ = 256
MOE_BM = 256
MOE_NB = (TOK * TOP_K + N_EXPERTS * (MOE_BM - 1) + MOE_BM - 1) // MOE_BM
MOE_ROWS = MOE_NB * MOE_BM

NT_DIMS = (((1,), (1,)), ((), ()))


def _rms(x, g):
    return x * lax.rsqrt(jnp.mean(x * x, axis=-1, keepdims=True) + NORM_EPS) * g


def _gelu(x):
    return 0.5 * x * (1.0 + lax.erf(x * (2.0 ** -0.5)))


def _standardize(x):
    mu = jnp.mean(x, axis=-1, keepdims=True)
    xc = x - mu
    var = jnp.mean(xc * xc, axis=-1, keepdims=True)
    return xc * lax.rsqrt(var + NORM_EPS)


def _ada_kernel(c_ref, w_ref, b_ref, o_ref):
    for r in range(2):
        cv = c_ref[r]
        act = cv * jax.nn.sigmoid(cv)
        for j in range(ADA_TN // LANES):
            cols = slice(j * LANES, (j + 1) * LANES)
            o_ref[0, r:r + 1, cols] = jnp.sum(w_ref[0, :, cols] * act, axis=0, keepdims=True) + b_ref[0, :, cols]


def _ada_call(c_rep, w_ada, b_ada):
    n_out = w_ada.shape[-1]
    return pl.pallas_call(
        _ada_kernel,
        grid=(DEPTH, n_out // ADA_TN),
        in_specs=[
            pl.BlockSpec((2, D_MODEL, LANES), lambda l, n: (0, 0, 0)),
            pl.BlockSpec((1, D_MODEL, ADA_TN), lambda l, n: (l, 0, n)),
            pl.BlockSpec((1, 1, ADA_TN), lambda l, n: (l, 0, n)),
        ],
        out_specs=pl.BlockSpec((1, 2, ADA_TN), lambda l, n: (l, 0, n)),
        out_shape=jax.ShapeDtypeStruct((DEPTH, 2, n_out), F32),
        compiler_params=pltpu.CompilerParams(dimension_semantics=("arbitrary", "arbitrary"),
                                             vmem_limit_bytes=VMEM_LIMIT),
        name="ada_ln",
    )(c_rep, w_ada, b_ada.reshape(DEPTH, 1, n_out))


def _rope(x, cos, sin_signed, half, lane):
    fwd = pltpu.roll(x, LANES - half, 1)
    bwd = pltpu.roll(x, half, 1)
    partner = jnp.where((lane & (2 * half - 1)) < half, fwd, bwd)
    return x * cos + partner * sin_signed


def _rope_t(x, cos, sin_signed, half):
    blocks = [x[b * half:(b + 1) * half] for b in range(HEAD_DIM // half)]
    partner = jnp.concatenate([blocks[b ^ 1] for b in range(len(blocks))], axis=0)
    return x * cos + partner * sin_signed


T_QA = 0
T_QB = T_QA + A_Q
T_VA = T_QB + B_QK
T_VB = T_VA + A_KV
T_ROWS = T_VB + B_V
R_KA = 0
R_KB = R_KA + A_KV
R_CU = R_KB + B_QK
R_CV = R_CU + C_U
R_COLS = R_CV + C_U
T_SEG = 512


def _inproj_kernel(x_ref, mod_ref, wr_ref, wt_ref, cosa_ref, sina_ref, cosb_ref, sinb_ref,
                   cosat_ref, sinat_ref, cosbt_ref, sinbt_ref, gqt_ref, gk_ref, ws_ref, bs_ref, gc_ref,
                   qar_ref, qaf_ref, ka_ref, va_ref, qbr_ref, qbf_ref, kb_ref, vb_ref, yc_ref, pt_sc):
    x = x_ref[...]
    h = (x * (1.0 + mod_ref[0, 1:2, :]) + mod_ref[0, 0:1, :]).astype(BF16)
    lane = lax.broadcasted_iota(jnp.int32, (TM, LANES), 1)
    cosa, sina = cosa_ref[...], sina_ref[...]
    cosb, sinb = cosb_ref[...], sinb_ref[...]
    scale_a = HEAD_DIM ** -0.5 * LOG2_E
    scale_b = B_QK_DIM ** -0.5 * LOG2_E

    for r0 in range(0, T_ROWS, T_SEG):
        n = min(T_SEG, T_ROWS - r0)
        pt_sc[r0:r0 + n, :] = lax.dot_general(wt_ref[r0:r0 + n, :], h, NT_DIMS, preferred_element_type=F32)
    for j in range(A_HEADS):
        q = pt_sc[T_QA + j * HEAD_DIM:T_QA + (j + 1) * HEAD_DIM, :]
        qn = q * lax.rsqrt(jnp.mean(q * q, axis=0, keepdims=True) + NORM_EPS) * (gqt_ref[...] * scale_a)
        qaf_ref[j] = qn.astype(BF16)
        qar_ref[j] = _rope_t(qn, cosat_ref[...], sinat_ref[...], 32).astype(BF16)
    zeros_half = jnp.zeros((B_QK_DIM, TM), BF16)
    for j in range(B_HEADS):
        qs = pt_sc[T_QB + j * HEAD_DIM:T_QB + (j + 1) * HEAD_DIM, :] * scale_b
        for q, dst in ((qs, qbf_ref), (_rope_t(qs, cosbt_ref[...], sinbt_ref[...], 16), qbr_ref)):
            qb = q.astype(BF16)
            dst[2 * j] = jnp.concatenate([qb[:B_QK_DIM], zeros_half], axis=0)
            dst[2 * j + 1] = jnp.concatenate([zeros_half, qb[B_QK_DIM:]], axis=0)
    for j in range(A_KV_HEADS):
        va_ref[j] = pt_sc[T_VA + j * HEAD_DIM:T_VA + (j + 1) * HEAD_DIM, :].astype(BF16)
    for j in range(B_HEADS):
        vb_ref[j] = pt_sc[T_VB + j * HEAD_DIM:T_VB + (j + 1) * HEAD_DIM, :].astype(BF16)

    def proj(col):
        p = jnp.dot(h, wr_ref[:, col:col + 2 * LANES], preferred_element_type=F32)
        return p[:, :LANES], p[:, LANES:]

    for j, k in enumerate(proj(R_KA)):
        ka_ref[j] = _rope(_rms(k, gk_ref[...]), cosa, sina, 32, lane).astype(BF16)
    for j2 in range(B_HEADS // 2):
        for j, k in zip((2 * j2, 2 * j2 + 1), proj(R_KB + j2 * 2 * LANES)):
            kb_ref[j] = _rope(k, cosb, sinb, 16, lane).astype(BF16)
    for j2 in range(C_GROUPS // 2):
        us = proj(R_CU + j2 * 2 * LANES)
        vs = proj(R_CV + j2 * 2 * LANES)
        for g, u, v in zip((2 * j2, 2 * j2 + 1), us, vs):
            u = _gelu(u)
            v = _standardize(_gelu(v)).astype(BF16)
            for c in range(TM // CHUNK):
                rows = slice(c * CHUNK, (c + 1) * CHUNK)
                mixed = jnp.dot(ws_ref[g], v[rows], preferred_element_type=F32) + bs_ref[g]
                yc_ref[rows, g * LANES:(g + 1) * LANES] = _rms(u[rows] * mixed, gc_ref[...]).astype(BF16)


def _inproj_call(xs, mod, w_rows, w_t, tabs, tabs_t, gq_t, gk, ws, bs, gc):
    def heads(n):
        return (jax.ShapeDtypeStruct((n, TOK, HEAD_DIM), BF16),
                pl.BlockSpec((n, TM, HEAD_DIM), lambda i: (0, i, 0)))

    def heads_t(n):
        return (jax.ShapeDtypeStruct((n, HEAD_DIM, TOK), BF16),
                pl.BlockSpec((n, HEAD_DIM, TM), lambda i: (0, 0, i)))

    outs = [heads_t(A_HEADS), heads_t(A_HEADS), heads(A_KV_HEADS), heads_t(A_KV_HEADS),
            heads_t(2 * B_HEADS), heads_t(2 * B_HEADS), heads(B_HEADS), heads_t(B_HEADS),
            (jax.ShapeDtypeStruct((TOK, C_U), BF16), pl.BlockSpec((TM, C_U), lambda i: (i, 0)))]
    tab_spec = pl.BlockSpec((TM, LANES), lambda i: (i, 0))
    tab_t_spec = pl.BlockSpec((HEAD_DIM, TM), lambda i: (0, i))
    vec_spec = pl.BlockSpec((1, LANES), lambda i: (0, 0))
    return pl.pallas_call(
        _inproj_kernel,
        grid=(N_TILES,),
        in_specs=[
            pl.BlockSpec((TM, D_MODEL), lambda i: (i, 0)),
            pl.BlockSpec((1, 6, D_MODEL), lambda i: (jnp.minimum(i, 1), 0, 0)),
            pl.BlockSpec((D_MODEL, R_COLS), lambda i: (0, 0), pipeline_mode=pl.Buffered(1)),
            pl.BlockSpec((T_ROWS, D_MODEL), lambda i: (0, 0), pipeline_mode=pl.Buffered(1)),
            tab_spec, tab_spec, tab_spec, tab_spec,
            tab_t_spec, tab_t_spec, tab_t_spec, tab_t_spec,
            pl.BlockSpec((HEAD_DIM, TM), lambda i: (0, 0)),
            vec_spec,
            pl.BlockSpec((C_GROUPS, CHUNK, CHUNK), lambda i: (0, 0, 0)),
            pl.BlockSpec((C_GROUPS, CHUNK, LANES), lambda i: (0, 0, 0)),
            vec_spec,
        ],
        out_specs=[o[1] for o in outs],
        out_shape=[o[0] for o in outs],
        scratch_shapes=[pltpu.VMEM((T_ROWS, TM), F32)],
        compiler_params=pltpu.CompilerParams(dimension_semantics=("arbitrary",), vmem_limit_bytes=VMEM_LIMIT),
        name="in_proj",
    )(xs, mod, w_rows, w_t, *tabs, *tabs_t, gq_t, gk, ws, bs, gc)


def _attn_kernel(*refs, groups, kv_heads, diff, lam_init):
    if diff:
        qr_ref, qf_ref, k_ref, vt_ref, lam_ref, g_ref, o_ref, m_sc, l_sc, acc_sc, s_sc, p_sc, a_sc, c_sc = refs
    else:
        qr_ref, qf_ref, k_ref, vt_ref, g_ref, o_ref, m_sc, l_sc, acc_sc, s_sc, p_sc, a_sc, c_sc = refs
    i = pl.program_id(1)
    n_chunks = SEQ // ATT_CK
    kv_of = lambda g: g // (groups // kv_heads)

    for g in range(groups):
        kc = k_ref[kv_of(g), 0:CTX, :]
        vc = vt_ref[kv_of(g), :, 0:CTX]
        s = jnp.dot(kc, qf_ref[g], preferred_element_type=F32)
        m0 = jnp.max(s, axis=0, keepdims=True)
        p = jnp.exp2(s - m0)
        m_sc[g] = m0
        l_sc[g] = jnp.sum(p, axis=0, keepdims=True)
        acc_sc[g] = jnp.dot(vc, p.astype(BF16), preferred_element_type=F32)

    def chunk_off(c):
        return pl.multiple_of(CTX + jnp.clip(c, 0, n_chunks - 1) * ATT_CK, CTX)

    def scores(g, c, slot):
        s = jnp.dot(k_ref[kv_of(g), pl.ds(chunk_off(c), ATT_CK), :], qr_ref[g], preferred_element_type=F32)
        s_sc[2 * g + slot] = s
        c_sc[2 * g + slot] = jnp.max(s, axis=0, keepdims=True)

    def softmax(g, slot):
        m_prev = m_sc[g]
        m_new = jnp.maximum(m_prev, c_sc[2 * g + slot])
        alpha = jnp.exp2(m_prev - m_new)
        p = jnp.exp2(s_sc[2 * g + slot] - m_new)
        l_sc[g] = alpha * l_sc[g] + jnp.sum(p, axis=0, keepdims=True)
        m_sc[g] = m_new
        a_sc[2 * g + slot] = alpha
        p_sc[2 * g + slot] = p.astype(BF16)

    def values(g, c, slot):
        acc_sc[g] = a_sc[2 * g + slot] * acc_sc[g] + jnp.dot(vt_ref[kv_of(g), :, pl.ds(chunk_off(c), ATT_CK)],
                                                             p_sc[2 * g + slot], preferred_element_type=F32)

    @pl.when(i >= CTX // ATT_TQ)
    def _latent_keys():
        for g in range(groups):
            scores(g, 0, 0)
            a_sc[2 * g + 1] = jnp.ones((1, ATT_TQ), F32)
            p_sc[2 * g + 1] = jnp.zeros((ATT_CK, ATT_TQ), BF16)

        def body(t, carry):
            c0 = 2 * t
            for g in range(groups):
                scores(g, c0 + 1, 1)
                softmax(g, 0)
                values(g, c0 - 1, 1)
            for g in range(groups):
                scores(g, c0 + 2, 0)
                softmax(g, 1)
                values(g, c0, 0)
            return carry

        lax.fori_loop(0, n_chunks // 2, body, 0)
        for g in range(groups):
            values(g, n_chunks - 1, 1)

    def finish(o_t, gain):
        o_t = o_t * lax.rsqrt(jnp.mean(o_t * o_t, axis=0, keepdims=True) + NORM_EPS) * gain
        return o_t.T.astype(BF16)

    if diff:
        lam = lam_ref[...]
        lam_val = (jnp.exp(jnp.sum(lam[0:1] * lam[1:2], axis=-1, keepdims=True))
                   - jnp.exp(jnp.sum(lam[2:3] * lam[3:4], axis=-1, keepdims=True)) + lam_init)
        for h in range(kv_heads):
            d = acc_sc[2 * h] / l_sc[2 * h] - lam_val * (acc_sc[2 * h + 1] / l_sc[2 * h + 1])
            o_ref[:, h * HEAD_DIM:(h + 1) * HEAD_DIM] = finish(d, g_ref[...] * (1.0 - lam_init))
    else:
        for g in range(groups):
            o_ref[:, g * HEAD_DIM:(g + 1) * HEAD_DIM] = finish(acc_sc[g] / l_sc[g], g_ref[...])


def _attn_call(qr, qf, k, vt, g_out_t, lam, *, groups, kv_heads, diff, lam_init, name):
    n_steps = k.shape[0] // kv_heads
    k_spec = pl.BlockSpec((kv_heads, TOK, HEAD_DIM), lambda h, i: (h, 0, 0))
    vt_spec = pl.BlockSpec((kv_heads, HEAD_DIM, TOK), lambda h, i: (h, 0, 0))
    q_spec = pl.BlockSpec((groups, HEAD_DIM, ATT_TQ), lambda h, i: (h, 0, i))
    in_specs = [q_spec, q_spec, k_spec, vt_spec]
    args = [qr, qf, k, vt]
    if diff:
        in_specs.append(pl.BlockSpec((4, B_QK_DIM), lambda h, i: (0, 0)))
        args.append(lam)
        out_w = kv_heads * HEAD_DIM
    else:
        out_w = groups * HEAD_DIM
    in_specs.append(pl.BlockSpec((HEAD_DIM, ATT_TQ), lambda h, i: (0, 0)))
    args.append(g_out_t)
    return pl.pallas_call(
        functools.partial(_attn_kernel, groups=groups, kv_heads=kv_heads, diff=diff, lam_init=lam_init),
        grid=(n_steps, TOK // ATT_TQ),
        in_specs=in_specs,
        out_specs=pl.BlockSpec((ATT_TQ, out_w), lambda h, i: (i, h)),
        out_shape=jax.ShapeDtypeStruct((TOK, n_steps * out_w), BF16),
        scratch_shapes=[pltpu.VMEM((groups, 1, ATT_TQ), F32), pltpu.VMEM((groups, 1, ATT_TQ), F32),
                        pltpu.VMEM((groups, HEAD_DIM, ATT_TQ), F32),
                        pltpu.VMEM((2 * groups, ATT_CK, ATT_TQ), F32),
                        pltpu.VMEM((2 * groups, ATT_CK, ATT_TQ), BF16),
                        pltpu.VMEM((2 * groups, 1, ATT_TQ), F32),
                        pltpu.VMEM((2 * groups, 1, ATT_TQ), F32)],
        compiler_params=pltpu.CompilerParams(dimension_semantics=("arbitrary", "arbitrary"),
                                             vmem_limit_bytes=VMEM_LIMIT),
        name=name,
    )(*args)


def _pair_max(vals):
    best = None
    for a in range(len(vals)):
        for b in range(a + 1, len(vals)):
            s = vals[a] + vals[b]
            best = s if best is None else jnp.maximum(best, s)
    return best


def _outproj_kernel(ya_ref, yb_ref, yc_ref, w_ref, x_ref, mod_ref, lng_ref, lnb_ref, wr_ref, rb_ref, tri_ref,
                    x1_ref, h2_ref, route_ref, cnt_ref, carry_sc):
    i = pl.program_id(0)

    @pl.when(i == 0)
    def _init():
        carry_sc[...] = jnp.zeros_like(carry_sc)

    y = (jnp.dot(ya_ref[...], w_ref[0:A_Q, :], preferred_element_type=F32)
         + jnp.dot(yb_ref[...], w_ref[A_Q:A_Q + B_V, :], preferred_element_type=F32)
         + jnp.dot(yc_ref[...], w_ref[A_Q + B_V:D_MIX, :], preferred_element_type=F32))
    z = DEEPNORM_ALPHA * x_ref[...] + mod_ref[0, 2:3, :] * y
    x1 = _standardize(z) * lng_ref[...] + lnb_ref[...]
    x1_ref[...] = x1
    h2 = x1 * (1.0 + mod_ref[0, 4:5, :]) + mod_ref[0, 3:4, :]
    h2_ref[...] = h2

    logits = lax.dot_general(wr_ref[...], h2, NT_DIMS, precision=lax.Precision.HIGHEST,
                             preferred_element_type=F32)
    scores = jax.nn.sigmoid(logits)
    biased = scores + rb_ref[...]
    b_rows = [biased[e:e + 1, :] for e in range(N_EXPERTS)]
    s_rows = [scores[e:e + 1, :] for e in range(N_EXPERTS)]
    group_score = [_pair_max(b_rows[g * EXPERTS_PER_GROUP:(g + 1) * EXPERTS_PER_GROUP])
                   for g in range(N_EXPERT_GROUPS)]
    best = group_score[0]
    grp = jnp.zeros_like(best)
    for g in range(1, N_EXPERT_GROUPS):
        better = group_score[g] > best
        grp = jnp.where(better, float(g), grp)
        best = jnp.where(better, group_score[g], best)

    def pick(rows_, j):
        out = rows_[j]
        for g in range(1, N_EXPERT_GROUPS):
            out = jnp.where(grp == float(g), rows_[g * EXPERTS_PER_GROUP + j], out)
        return out

    vb = [pick(b_rows, j) for j in range(EXPERTS_PER_GROUP)]
    vs = [pick(s_rows, j) for j in range(EXPERTS_PER_GROUP)]
    chosen = []
    for j in range(EXPERTS_PER_GROUP):
        rank = jnp.zeros_like(best)
        for k in range(EXPERTS_PER_GROUP):
            if k == j:
                continue
            ahead = (vb[k] > vb[j]) | ((vb[k] == vb[j]) if k < j else False)
            rank = rank + jnp.where(ahead, 1.0, 0.0)
        chosen.append(rank < float(TOP_K))
    loc1 = jnp.full_like(best, float(EXPERTS_PER_GROUP))
    loc2 = jnp.full_like(best, -1.0)
    for j in range(EXPERTS_PER_GROUP):
        loc1 = jnp.where(chosen[j], jnp.minimum(loc1, float(j)), loc1)
        loc2 = jnp.where(chosen[j], jnp.maximum(loc2, float(j)), loc2)
    g1 = jnp.zeros_like(best)
    g2 = jnp.zeros_like(best)
    for j in range(EXPERTS_PER_GROUP):
        g1 = jnp.where(loc1 == float(j), vs[j], g1)
        g2 = jnp.where(loc2 == float(j), vs[j], g2)
    gsum = g1 + g2
    e1 = grp * float(EXPERTS_PER_GROUP) + loc1
    e2 = grp * float(EXPERTS_PER_GROUP) + loc2

    eidx = lax.broadcasted_iota(jnp.int32, (N_EXPERTS, TM), 0).astype(F32)
    is1 = eidx == e1
    is2 = eidx == e2
    sel = jnp.where(is1 | is2, 1.0, 0.0)
    before = jnp.dot(sel.astype(BF16), tri_ref[...], preferred_element_type=F32) + carry_sc[...]
    pos1 = jnp.sum(jnp.where(is1, before, 0.0), axis=0, keepdims=True)
    pos2 = jnp.sum(jnp.where(is2, before, 0.0), axis=0, keepdims=True)
    carry = carry_sc[...] + jnp.sum(sel, axis=1, keepdims=True)
    carry_sc[...] = carry
    cnt_ref[...] = carry[:, :LANES]

    route_ref[0:1, :] = e1
    route_ref[1:2, :] = e2
    route_ref[2:3, :] = pos1
    route_ref[3:4, :] = pos2
    route_ref[4:5, :] = g1 / gsum
    route_ref[5:6, :] = g2 / gsum
    route_ref[6:8, :] = jnp.zeros((2, TM), F32)


def _outproj_call(ya, yb, yc, w_out, xs, mod, ln_g, ln_b, wr_t, rb_rep, tri):
    row_spec = lambda w: pl.BlockSpec((TM, w), lambda i: (i, 0))
    vec_spec = pl.BlockSpec((1, D_MODEL), lambda i: (0, 0))
    return pl.pallas_call(
        _outproj_kernel,
        grid=(N_TILES,),
        in_specs=[
            row_spec(A_Q), row_spec(B_V), row_spec(C_U),
            pl.BlockSpec((D_MIX, D_MODEL), lambda i: (0, 0), pipeline_mode=pl.Buffered(1)),
            row_spec(D_MODEL),
            pl.BlockSpec((1, 6, D_MODEL), lambda i: (jnp.minimum(i, 1), 0, 0)),
            vec_spec, vec_spec,
            pl.BlockSpec((N_EXPERTS, D_MODEL), lambda i: (0, 0)),
            pl.BlockSpec((N_EXPERTS, TM), lambda i: (0, 0)),
            pl.BlockSpec((TM, TM), lambda i: (0, 0)),
        ],
        out_specs=[row_spec(D_MODEL), row_spec(D_MODEL),
                   pl.BlockSpec((8, TM), lambda i: (0, i)),
                   pl.BlockSpec((N_EXPERTS, LANES), lambda i: (0, 0))],
        out_shape=[jax.ShapeDtypeStruct((TOK, D_MODEL), F32), jax.ShapeDtypeStruct((TOK, D_MODEL), F32),
                   jax.ShapeDtypeStruct((8, TOK), F32), jax.ShapeDtypeStruct((N_EXPERTS, LANES), F32)],
        scratch_shapes=[pltpu.VMEM((N_EXPERTS, TM), F32)],
        compiler_params=pltpu.CompilerParams(dimension_semantics=("arbitrary",), vmem_limit_bytes=VMEM_LIMIT),
        name="out_proj",
    )(ya, yb, yc, w_out, xs, mod, ln_g, ln_b, wr_t, rb_rep, tri)


def _row_copy(src_ref, src_row, dst_ref, dst_row, sem):
    return pltpu.make_async_copy(src_ref.at[pl.ds(src_row, 1)], dst_ref.at[pl.ds(dst_row, 1)], sem)


def _slot_source_kernel(dest_ref, src_ref):
    def clear(s, carry):
        src_ref[s] = 0
        return carry

    def scatter(t, carry):
        src_ref[dest_ref[t]] = t
        src_ref[dest_ref[TOK + t]] = t
        return carry

    lax.fori_loop(0, MOE_ROWS, clear, 0, unroll=8)
    lax.fori_loop(0, TOK, scatter, 0, unroll=8)


def _slot_source_call(dest):
    return pl.pallas_call(
        _slot_source_kernel,
        in_specs=[pl.BlockSpec(memory_space=pltpu.SMEM)],
        out_specs=pl.BlockSpec(memory_space=pltpu.SMEM),
        out_shape=jax.ShapeDtypeStruct((MOE_ROWS,), jnp.int32),
        name="moe_slot_source",
    )(dest)


def _moe_kernel(be_ref, na_ref, src_ref, h_ref, wg_ref, wu_ref, wd_ref, o_ref, x_sc, sems):
    del be_ref
    b = pl.program_id(0)
    n_active = na_ref[0]

    def start_gather(block, slot):
        def body(r, carry):
            _row_copy(h_ref, src_ref[block * MOE_BM + r], x_sc.at[slot], r, sems.at[slot]).start()
            return carry
        lax.fori_loop(0, MOE_BM, body, 0, unroll=8)

    def wait_gather(slot):
        def body(r, carry):
            _row_copy(h_ref, 0, x_sc.at[slot], 0, sems.at[slot]).wait()
            return carry
        lax.fori_loop(0, MOE_BM, body, 0, unroll=8)

    @pl.when(b == 0)
    def _first():
        start_gather(0, 0)

    @pl.when(b + 1 < n_active)
    def _prefetch_next():
        start_gather(b + 1, (b + 1) % 2)

    @pl.when(b >= n_active)
    def _unused_block():
        o_ref[...] = jnp.zeros_like(o_ref)

    @pl.when(b < n_active)
    def _block():
        slot = b % 2
        wait_gather(slot)
        xb = x_sc[slot].astype(BF16)
        gate = jnp.dot(xb, wg_ref[0], preferred_element_type=F32)
        up = jnp.dot(xb, wu_ref[0], preferred_element_type=F32)
        act = (gate * jax.nn.sigmoid(gate) * up).astype(BF16)
        o_ref[...] = jnp.dot(act, wd_ref[0], preferred_element_type=F32)


def _moe_call(block_e, n_active, src, h2, w_gate, w_up, w_down):
    w_map = lambda b, be, na, sr: (be[b], 0, 0)
    return pl.pallas_call(
        _moe_kernel,
        grid_spec=pltpu.PrefetchScalarGridSpec(
            num_scalar_prefetch=3,
            grid=(MOE_NB,),
            in_specs=[pl.BlockSpec(memory_space=pl.ANY),
                      pl.BlockSpec((1, D_MODEL, D_EXPERT), w_map),
                      pl.BlockSpec((1, D_MODEL, D_EXPERT), w_map),
                      pl.BlockSpec((1, D_EXPERT, D_MODEL), w_map)],
            out_specs=pl.BlockSpec((MOE_BM, D_MODEL), lambda b, be, na, sr: (b, 0)),
            scratch_shapes=[pltpu.VMEM((2, MOE_BM, D_MODEL), F32), pltpu.SemaphoreType.DMA((2,))],
        ),
        out_shape=jax.ShapeDtypeStruct((MOE_ROWS, D_MODEL), F32),
        compiler_params=pltpu.CompilerParams(dimension_semantics=("arbitrary",), vmem_limit_bytes=VMEM_LIMIT),
        name="moe_experts",
    )(block_e, n_active, src, h2, w_gate, w_up, w_down)


def _combine_kernel(dest_ref, ys_ref, x1_ref, g1_ref, g2_ref, mod_ref, lng_ref, lnb_ref, o_ref, rows_sc, sem):
    base = pl.program_id(0) * TM

    def start(t, carry):
        tok = base + t
        _row_copy(ys_ref, dest_ref[tok], rows_sc.at[0], t, sem).start()
        _row_copy(ys_ref, dest_ref[TOK + tok], rows_sc.at[1], t, sem).start()
        return carry

    def wait(t, carry):
        _row_copy(ys_ref, 0, rows_sc.at[0], 0, sem).wait()
        _row_copy(ys_ref, 0, rows_sc.at[1], 0, sem).wait()
        return carry

    lax.fori_loop(0, TM, start, 0)
    lax.fori_loop(0, TM, wait, 0)
    y = g1_ref[...] * rows_sc[0] + g2_ref[...] * rows_sc[1]
    z = DEEPNORM_ALPHA * x1_ref[...] + mod_ref[0, 5:6, :] * y
    o_ref[...] = _standardize(z) * lng_ref[...] + lnb_ref[...]


def _combine_call(dest, ys, x1, g1, g2, mod, ln_g, ln_b):
    row_spec = lambda w: pl.BlockSpec((TM, w), lambda i, d: (i, 0))
    vec_spec = pl.BlockSpec((1, D_MODEL), lambda i, d: (0, 0))
    return pl.pallas_call(
        _combine_kernel,
        grid_spec=pltpu.PrefetchScalarGridSpec(
            num_scalar_prefetch=1,
            grid=(N_TILES,),
            in_specs=[pl.BlockSpec(memory_space=pl.ANY), row_spec(D_MODEL), row_spec(1), row_spec(1),
                      pl.BlockSpec((1, 6, D_MODEL), lambda i, d: (jnp.minimum(i, 1), 0, 0)),
                      vec_spec, vec_spec],
            out_specs=row_spec(D_MODEL),
            scratch_shapes=[pltpu.VMEM((2, TM, D_MODEL), F32), pltpu.SemaphoreType.DMA(())],
        ),
        out_shape=jax.ShapeDtypeStruct((TOK, D_MODEL), F32),
        compiler_params=pltpu.CompilerParams(dimension_semantics=("arbitrary",), vmem_limit_bytes=VMEM_LIMIT),
        name="moe_combine",
    )(dest, ys, x1, g1, g2, mod, ln_g, ln_b)


def _rope_tables(dim):
    n_freq = dim // 4
    inv = ROPE_THETA ** (-jnp.arange(n_freq, dtype=F32) / n_freq)
    t = jnp.arange(SEQ)
    rows = (t // GRID_W).astype(F32)
    cols = (t % GRID_W).astype(F32)
    lane = jnp.arange(LANES)
    within = lane % dim
    use_col = within >= dim // 2
    freq = within % n_freq
    second_half = (within % (dim // 2)) >= n_freq
    pos = jnp.where(use_col[None, :], cols[:, None], rows[:, None])
    ang = pos * inv[freq][None, :]
    cos = jnp.cos(ang)
    sin = jnp.where(second_half[None, :], jnp.sin(ang), -jnp.sin(ang))
    cos = jnp.concatenate([jnp.ones((CTX, LANES), F32), cos], axis=0)
    sin = jnp.concatenate([jnp.zeros((CTX, LANES), F32), sin], axis=0)
    return cos, sin


def kernel(x, c, ctx, c_ctx, w_ada, b_ada, w_in, w_out, a_q_norm, a_k_norm, a_out_norm, b_lambda, b_out_norm,
           c_spatial, c_spatial_bias, c_out_norm, ln1_g, ln1_b, ln2_g, ln2_b, w_router, router_bias,
           w_gate, w_up, w_down):
    assert x.shape == (1, SEQ, D_MODEL) and ctx.shape == (1, CTX, D_MODEL)
    cos_a, sin_a = _rope_tables(HEAD_DIM)
    cos_b, sin_b = _rope_tables(B_QK_DIM)
    tabs = (cos_a, sin_a, cos_b, sin_b)

    c_rep = jnp.broadcast_to(jnp.stack([c_ctx, c[0]])[:, :, None], (2, D_MODEL, LANES))
    mods = _ada_call(c_rep, w_ada, b_ada).reshape(DEPTH, 2, 6, D_MODEL)

    w_in_b = w_in.astype(BF16)
    cols = lambda off, n: w_in_b[:, :, off:off + n]
    w_rows = jnp.concatenate([cols(OFF_AK, A_KV), cols(OFF_BK, B_QK), cols(OFF_CU, C_U), cols(OFF_CV, C_U)], axis=2)
    w_t = jnp.swapaxes(jnp.concatenate([cols(OFF_AQ, A_Q), cols(OFF_BQ, B_QK), cols(OFF_AV, A_KV),
                                        cols(OFF_BV, B_V)], axis=2), 1, 2)
    tabs_t = tuple(t.T for t in tabs)
    col_rep = lambda a: jnp.broadcast_to(a[:, :, None], (DEPTH, HEAD_DIM, TM))
    gq_t, ga_t, gb_t = col_rep(a_q_norm), col_rep(a_out_norm), col_rep(b_out_norm)
    w_out_b = w_out.astype(BF16)
    ws_b = c_spatial.astype(BF16)
    bs_rep = jnp.broadcast_to(c_spatial_bias[:, :, :, None], (DEPTH, C_GROUPS, CHUNK, LANES))
    w_gate_b = w_gate.astype(BF16)
    w_up_b = w_up.astype(BF16)
    w_down_b = w_down.astype(BF16)
    wr_t = w_router.T
    rb_rep = jnp.broadcast_to(router_bias[:, None], (N_EXPERTS, TM))
    tri = jnp.triu(jnp.ones((TM, TM), BF16), k=1)

    xs = jnp.concatenate([ctx[0], x[0]], axis=0)
    for l in range(DEPTH):
        lam_init = 0.8 - 0.6 * math.exp(-0.3 * l)
        mod = mods[l]
        vec = lambda a: a[l].reshape(1, -1)
        qar, qaf, ka, va, qbr, qbf, kb, vb, yc = _inproj_call(
            xs, mod, w_rows[l], w_t[l], tabs, tabs_t, gq_t[l], vec(a_k_norm), ws_b[l], bs_rep[l], vec(c_out_norm))
        ya = _attn_call(qar, qaf, ka, va, ga_t[l], None, groups=A_GROUP, kv_heads=1, diff=False,
                        lam_init=lam_init, name="attn_gqa")
        yb = _attn_call(qbr, qbf, kb, vb, gb_t[l], b_lambda[l], groups=4, kv_heads=2, diff=True,
                        lam_init=lam_init, name="attn_diff")
        x1, h2, route, counts = _outproj_call(ya, yb, yc, w_out_b[l], xs, mod, vec(ln1_g), vec(ln1_b),
                                              wr_t, rb_rep, tri)

        cnt = counts[:, 0].astype(jnp.int32)
        padded = (cnt + MOE_BM - 1) // MOE_BM * MOE_BM
        ends = jnp.cumsum(padded)
        starts = ends - padded
        e1 = route[0].astype(jnp.int32)
        e2 = route[1].astype(jnp.int32)
        dest = jnp.concatenate([starts[e1] + route[2].astype(jnp.int32), starts[e2] + route[3].astype(jnp.int32)])
        n_active = (ends[-1] // MOE_BM).astype(jnp.int32)
        blk = jnp.minimum(jnp.arange(MOE_NB, dtype=jnp.int32), n_active - 1)
        block_e = jnp.minimum(jnp.searchsorted(ends, blk * MOE_BM, side="right"), N_EXPERTS - 1).astype(jnp.int32)

        src = _slot_source_call(dest)
        ys = _moe_call(block_e, n_active.reshape(1), src, h2, w_gate_b[l], w_up_b[l], w_down_b[l])
        xs = _combine_call(dest, ys, x1, route[4].reshape(TOK, 1), route[5].reshape(TOK, 1), mod,
                           vec(ln2_g), vec(ln2_b))
    return xs[CTX:].reshape(1, SEQ, D_MODEL)
```

```python
import functools
import math

import jax
import jax.numpy as jnp
from jax import lax
from jax.experimental import pallas as pl
from jax.experimental.pallas import tpu as pltpu

F32 = jnp.float32
BF16 = jnp.bfloat16

D_MODEL = 2048
SEQ = 8192
CTX = 256
TOK = CTX + SEQ
DEPTH = 4
GRID_W = 64

HEAD_DIM = 128
A_HEADS = 8
A_KV_HEADS = 2
A_GROUP = A_HEADS // A_KV_HEADS
B_HEADS = 4
B_QK_DIM = 64
C_GROUPS = 4
CHUNK = 128
ROPE_THETA = 10000.0

A_Q = A_HEADS * HEAD_DIM
A_KV = A_KV_HEADS * HEAD_DIM
B_QK = B_HEADS * HEAD_DIM
B_V = B_HEADS * HEAD_DIM
C_U = C_GROUPS * HEAD_DIM
D_IN = A_Q + 2 * A_KV + 2 * B_QK + B_V + 2 * C_U
D_MIX = A_Q + B_V + C_U
OFF_AQ = 0
OFF_AK = OFF_AQ + A_Q
OFF_AV = OFF_AK + A_KV
OFF_BQ = OFF_AV + A_KV
OFF_BK = OFF_BQ + B_QK
OFF_BV = OFF_BK + B_QK
OFF_CU = OFF_BV + B_V
OFF_CV = OFF_CU + C_U

N_EXPERTS = 16
N_EXPERT_GROUPS = 4
EXPERTS_PER_GROUP = 4
TOP_K = 2
D_EXPERT = 1024

DEEPNORM_ALPHA = (2 * DEPTH) ** 0.25
NORM_EPS = 1e-6
LOG2_E = math.log2(math.e)

LANES = 128
VMEM_LIMIT = 60 * 1024 * 1024

TM = 256
N_TILES = TOK // TM
ADA_TN = 512
LN_ROWS = 16
ATT_CK = 512
ATT_UNROLL = 2
ATT_TQ = 256
MOE_BM = 256
MOE_NB = (TOK * TOP_K + N_EXPERTS * (MOE_BM - 1) + MOE_BM - 1) // MOE_BM
MOE_ROWS = MOE_NB * MOE_BM

NT_DIMS = (((1,), (1,)), ((), ()))


def _rms(x, g):
    return x * lax.rsqrt(jnp.mean(x * x, axis=-1, keepdims=True) + NORM_EPS) * g


def _gelu(x):
    return 0.5 * x * (1.0 + lax.erf(x * (2.0 ** -0.5)))


def _standardize(x):
    mu = jnp.mean(x, axis=-1, keepdims=True)
    xc = x - mu
    var = jnp.mean(xc * xc, axis=-1, keepdims=True)
    return xc * lax.rsqrt(var + NORM_EPS)


def _ada_kernel(c_ref, w_ref, b_ref, o_ref):
    for r in range(2):
        cv = c_ref[r]
        act = cv * jax.nn.sigmoid(cv)
        for j in range(ADA_TN // LANES):
            cols = slice(j * LANES, (j + 1) * LANES)
            o_ref[0, r:r + 1, cols] = jnp.sum(w_ref[0, :, cols] * act, axis=0, keepdims=True) + b_ref[0, :, cols]


def _ada_call(c_rep, w_ada, b_ada):
    n_out = w_ada.shape[-1]
    return pl.pallas_call(
        _ada_kernel,
        grid=(DEPTH, n_out // ADA_TN),
        in_specs=[
            pl.BlockSpec((2, D_MODEL, LANES), lambda l, n: (0, 0, 0)),
            pl.BlockSpec((1, D_MODEL, ADA_TN), lambda l, n: (l, 0, n)),
            pl.BlockSpec((1, 1, ADA_TN), lambda l, n: (l, 0, n)),
        ],
        out_specs=pl.BlockSpec((1, 2, ADA_TN), lambda l, n: (l, 0, n)),
        out_shape=jax.ShapeDtypeStruct((DEPTH, 2, n_out), F32),
        compiler_params=pltpu.CompilerParams(dimension_semantics=("arbitrary", "arbitrary"),
                                             vmem_limit_bytes=VMEM_LIMIT),
        name="ada_ln",
    )(c_rep, w_ada, b_ada.reshape(DEPTH, 1, n_out))


def _rope(x, cos, sin_signed, half, lane):
    fwd = pltpu.roll(x, LANES - half, 1)
    bwd = pltpu.roll(x, half, 1)
    partner = jnp.where((lane & (2 * half - 1)) < half, fwd, bwd)
    return x * cos + partner * sin_signed


def _rope_t(x, cos, sin_signed, half):
    blocks = [x[b * half:(b + 1) * half] for b in range(HEAD_DIM // half)]
    partner = jnp.concatenate([blocks[b ^ 1] for b in range(len(blocks))], axis=0)
    return x * cos + partner * sin_signed


T_QA = 0
T_QB = T_QA + A_Q
T_VA = T_QB + B_QK
T_VB = T_VA + A_KV
T_ROWS = T_VB + B_V
R_KA = 0
R_KB = R_KA + A_KV
R_CU = R_KB + B_QK
R_CV = R_CU + C_U
R_COLS = R_CV + C_U
T_SEG = 512


def _inproj_kernel(x_ref, mod_ref, wr_ref, wt_ref, cosa_ref, sina_ref, cosb_ref, sinb_ref,
                   cosat_ref, sinat_ref, cosbt_ref, sinbt_ref, gqt_ref, gk_ref, ws_ref, bs_ref, gc_ref,
                   qar_ref, qaf_ref, ka_ref, va_ref, qbr_ref, qbf_ref, kb_ref, vb_ref, yc_ref, pt_sc):
    x = x_ref[...]
    h = (x * (1.0 + mod_ref[0, 1:2, :]) + mod_ref[0, 0:1, :]).astype(BF16)
    lane = lax.broadcasted_iota(jnp.int32, (TM, LANES), 1)
    cosa, sina = cosa_ref[...], sina_ref[...]
    cosb, sinb = cosb_ref[...], sinb_ref[...]
    scale_a = HEAD_DIM ** -0.5 * LOG2_E
    scale_b = B_QK_DIM ** -0.5 * LOG2_E

    for r0 in range(0, T_ROWS, T_SEG):
        n = min(T_SEG, T_ROWS - r0)
        pt_sc[r0:r0 + n, :] = lax.dot_general(wt_ref[r0:r0 + n, :], h, NT_DIMS, preferred_element_type=F32)
    for j in range(A_HEADS):
        q = pt_sc[T_QA + j * HEAD_DIM:T_QA + (j + 1) * HEAD_DIM, :]
        qn = q * lax.rsqrt(jnp.mean(q * q, axis=0, keepdims=True) + NORM_EPS) * (gqt_ref[...] * scale_a)
        qaf_ref[j] = qn.astype(BF16)
        qar_ref[j] = _rope_t(qn, cosat_ref[...], sinat_ref[...], 32).astype(BF16)
    zeros_half = jnp.zeros((B_QK_DIM, TM), BF16)
    for j in range(B_HEADS):
        qs = pt_sc[T_QB + j * HEAD_DIM:T_QB + (j + 1) * HEAD_DIM, :] * scale_b
        for q, dst in ((qs, qbf_ref), (_rope_t(qs, cosbt_ref[...], sinbt_ref[...], 16), qbr_ref)):
            qb = q.astype(BF16)
            dst[2 * j] = jnp.concatenate([qb[:B_QK_DIM], zeros_half], axis=0)
            dst[2 * j + 1] = jnp.concatenate([zeros_half, qb[B_QK_DIM:]], axis=0)
    for j in range(A_KV_HEADS):
        va_ref[j] = pt_sc[T_VA + j * HEAD_DIM:T_VA + (j + 1) * HEAD_DIM, :].astype(BF16)
    for j in range(B_HEADS):
        vb_ref[j] = pt_sc[T_VB + j * HEAD_DIM:T_VB + (j + 1) * HEAD_DIM, :].astype(BF16)

    def proj(col):
        p = jnp.dot(h, wr_ref[:, col:col + 2 * LANES], preferred_element_type=F32)
        return p[:, :LANES], p[:, LANES:]

    for j, k in enumerate(proj(R_KA)):
        ka_ref[j] = _rope(_rms(k, gk_ref[...]), cosa, sina, 32, lane).astype(BF16)
    for j2 in range(B_HEADS // 2):
        for j, k in zip((2 * j2, 2 * j2 + 1), proj(R_KB + j2 * 2 * LANES)):
            kb_ref[j] = _rope(k, cosb, sinb, 16, lane).astype(BF16)
    for j2 in range(C_GROUPS // 2):
        us = proj(R_CU + j2 * 2 * LANES)
        vs = proj(R_CV + j2 * 2 * LANES)
        for g, u, v in zip((2 * j2, 2 * j2 + 1), us, vs):
            u = _gelu(u)
            v = _standardize(_gelu(v)).astype(BF16)
            for c in range(TM // CHUNK):
                rows = slice(c * CHUNK, (c + 1) * CHUNK)
                mixed = jnp.dot(ws_ref[g], v[rows], preferred_element_type=F32) + bs_ref[g]
                yc_ref[rows, g * LANES:(g + 1) * LANES] = _rms(u[rows] * mixed, gc_ref[...]).astype(BF16)


def _inproj_call(xs, mod, w_rows, w_t, tabs, tabs_t, gq_t, gk, ws, bs, gc):
    def heads(n):
        return (jax.ShapeDtypeStruct((n, TOK, HEAD_DIM), BF16),
                pl.BlockSpec((n, TM, HEAD_DIM), lambda i: (0, i, 0)))

    def heads_t(n):
        return (jax.ShapeDtypeStruct((n, HEAD_DIM, TOK), BF16),
                pl.BlockSpec((n, HEAD_DIM, TM), lambda i: (0, 0, i)))

    outs = [heads_t(A_HEADS), heads_t(A_HEADS), heads(A_KV_HEADS), heads_t(A_KV_HEADS),
            heads_t(2 * B_HEADS), heads_t(2 * B_HEADS), heads(B_HEADS), heads_t(B_HEADS),
            (jax.ShapeDtypeStruct((TOK, C_U), BF16), pl.BlockSpec((TM, C_U), lambda i: (i, 0)))]
    tab_spec = pl.BlockSpec((TM, LANES), lambda i: (i, 0))
    tab_t_spec = pl.BlockSpec((HEAD_DIM, TM), lambda i: (0, i))
    vec_spec = pl.BlockSpec((1, LANES), lambda i: (0, 0))
    return pl.pallas_call(
        _inproj_kernel,
        grid=(N_TILES,),
        in_specs=[
            pl.BlockSpec((TM, D_MODEL), lambda i: (i, 0)),
            pl.BlockSpec((1, 6, D_MODEL), lambda i: (jnp.minimum(i, 1), 0, 0)),
            pl.BlockSpec((D_MODEL, R_COLS), lambda i: (0, 0), pipeline_mode=pl.Buffered(1)),
            pl.BlockSpec((T_ROWS, D_MODEL), lambda i: (0, 0), pipeline_mode=pl.Buffered(1)),
            tab_spec, tab_spec, tab_spec, tab_spec,
            tab_t_spec, tab_t_spec, tab_t_spec, tab_t_spec,
            pl.BlockSpec((HEAD_DIM, TM), lambda i: (0, 0)),
            vec_spec,
            pl.BlockSpec((C_GROUPS, CHUNK, CHUNK), lambda i: (0, 0, 0)),
            pl.BlockSpec((C_GROUPS, CHUNK, LANES), lambda i: (0, 0, 0)),
            vec_spec,
        ],
        out_specs=[o[1] for o in outs],
        out_shape=[o[0] for o in outs],
        scratch_shapes=[pltpu.VMEM((T_ROWS, TM), F32)],
        compiler_params=pltpu.CompilerParams(dimension_semantics=("arbitrary",), vmem_limit_bytes=VMEM_LIMIT),
        name="in_proj",
    )(xs, mod, w_rows, w_t, *tabs, *tabs_t, gq_t, gk, ws, bs, gc)


def _attn_kernel(*refs, groups, kv_heads, diff, lam_init):
    if diff:
        qr_ref, qf_ref, k_ref, vt_ref, lam_ref, g_ref, o_ref, m_sc, l_sc, acc_sc, s_sc, p_sc, a_sc, c_sc = refs
    else:
        qr_ref, qf_ref, k_ref, vt_ref, g_ref, o_ref, m_sc, l_sc, acc_sc, s_sc, p_sc, a_sc, c_sc = refs
    i = pl.program_id(1)
    n_chunks = SEQ // ATT_CK
    kv_of = lambda g: g // (groups // kv_heads)

    for g in range(groups):
        kc = k_ref[kv_of(g), 0:CTX, :]
        vc = vt_ref[kv_of(g), :, 0:CTX]
        s = jnp.dot(kc, qf_ref[g], preferred_element_type=F32)
        m0 = jnp.max(s, axis=0, keepdims=True)
        p = jnp.exp2(s - m0)
        m_sc[g] = m0
        l_sc[g] = jnp.sum(p, axis=0, keepdims=True)
        acc_sc[g] = jnp.dot(vc, p.astype(BF16), preferred_element_type=F32)

    def chunk_off(c):
        return pl.multiple_of(CTX + jnp.clip(c, 0, n_chunks - 1) * ATT_CK, CTX)

    def scores(g, c, slot):
        s = jnp.dot(k_ref[kv_of(g), pl.ds(chunk_off(c), ATT_CK), :], qr_ref[g], preferred_element_type=F32)
        s_sc[2 * g + slot] = s
        c_sc[2 * g + slot] = jnp.max(s, axis=0, keepdims=True)

    def softmax(g, slot):
        m_prev = m_sc[g]
        m_new = jnp.maximum(m_prev, c_sc[2 * g + slot])
        alpha = jnp.exp2(m_prev - m_new)
        p = jnp.exp2(s_sc[2 * g + slot] - m_new)
        l_sc[g] = alpha * l_sc[g] + jnp.sum(p, axis=0, keepdims=True)
        m_sc[g] = m_new
        a_sc[2 * g + slot] = alpha
        p_sc[2 * g + slot] = p.astype(BF16)

    def values(g, c, slot):
        acc_sc[g] = a_sc[2 * g + slot] * acc_sc[g] + jnp.dot(vt_ref[kv_of(g), :, pl.ds(chunk_off(c), ATT_CK)],
                                                             p_sc[2 * g + slot], preferred_element_type=F32)

    @pl.when(i >= CTX // ATT_TQ)
    def _latent_keys():
        for g in range(groups):
            scores(g, 0, 0)
            a_sc[2 * g + 1] = jnp.ones((1, ATT_TQ), F32)
            p_sc[2 * g + 1] = jnp.zeros((ATT_CK, ATT_TQ), BF16)

        def body(t, carry):
            for u in range(ATT_UNROLL):
                c0 = 2 * (ATT_UNROLL * t + u)
                for g in range(groups):
                    scores(g, c0 + 1, 1)
                    softmax(g, 0)
                    values(g, c0 - 1, 1)
                for g in range(groups):
                    scores(g, c0 + 2, 0)
                    softmax(g, 1)
                    values(g, c0, 0)
            return carry

        lax.fori_loop(0, n_chunks // (2 * ATT_UNROLL), body, 0)
        for g in range(groups):
            values(g, n_chunks - 1, 1)

    def finish(o_t, gain):
        o_t = o_t * lax.rsqrt(jnp.mean(o_t * o_t, axis=0, keepdims=True) + NORM_EPS) * gain
        return o_t.T.astype(BF16)

    if diff:
        lam = lam_ref[...]
        lam_val = (jnp.exp(jnp.sum(lam[0:1] * lam[1:2], axis=-1, keepdims=True))
                   - jnp.exp(jnp.sum(lam[2:3] * lam[3:4], axis=-1, keepdims=True)) + lam_init)
        for h in range(kv_heads):
            d = acc_sc[2 * h] / l_sc[2 * h] - lam_val * (acc_sc[2 * h + 1] / l_sc[2 * h + 1])
            o_ref[:, h * HEAD_DIM:(h + 1) * HEAD_DIM] = finish(d, g_ref[...] * (1.0 - lam_init))
    else:
        for g in range(groups):
            o_ref[:, g * HEAD_DIM:(g + 1) * HEAD_DIM] = finish(acc_sc[g] / l_sc[g], g_ref[...])


def _attn_call(qr, qf, k, vt, g_out_t, lam, *, groups, kv_heads, diff, lam_init, name):
    n_steps = k.shape[0] // kv_heads
    k_spec = pl.BlockSpec((kv_heads, TOK, HEAD_DIM), lambda h, i: (h, 0, 0))
    vt_spec = pl.BlockSpec((kv_heads, HEAD_DIM, TOK), lambda h, i: (h, 0, 0))
    q_spec = pl.BlockSpec((groups, HEAD_DIM, ATT_TQ), lambda h, i: (h, 0, i))
    in_specs = [q_spec, q_spec, k_spec, vt_spec]
    args = [qr, qf, k, vt]
    if diff:
        in_specs.append(pl.BlockSpec((4, B_QK_DIM), lambda h, i: (0, 0)))
        args.append(lam)
        out_w = kv_heads * HEAD_DIM
    else:
        out_w = groups * HEAD_DIM
    in_specs.append(pl.BlockSpec((HEAD_DIM, ATT_TQ), lambda h, i: (0, 0)))
    args.append(g_out_t)
    return pl.pallas_call(
        functools.partial(_attn_kernel, groups=groups, kv_heads=kv_heads, diff=diff, lam_init=lam_init),
        grid=(n_steps, TOK // ATT_TQ),
        in_specs=in_specs,
        out_specs=pl.BlockSpec((ATT_TQ, out_w), lambda h, i: (i, h)),
        out_shape=jax.ShapeDtypeStruct((TOK, n_steps * out_w), BF16),
        scratch_shapes=[pltpu.VMEM((groups, 1, ATT_TQ), F32), pltpu.VMEM((groups, 1, ATT_TQ), F32),
                        pltpu.VMEM((groups, HEAD_DIM, ATT_TQ), F32),
                        pltpu.VMEM((2 * groups, ATT_CK, ATT_TQ), F32),
                        pltpu.VMEM((2 * groups, ATT_CK, ATT_TQ), BF16),
                        pltpu.VMEM((2 * groups, 1, ATT_TQ), F32),
                        pltpu.VMEM((2 * groups, 1, ATT_TQ), F32)],
        compiler_params=pltpu.CompilerParams(dimension_semantics=("arbitrary", "arbitrary"),
                                             vmem_limit_bytes=VMEM_LIMIT),
        name=name,
    )(*args)


def _pair_max(vals):
    best = None
    for a in range(len(vals)):
        for b in range(a + 1, len(vals)):
            s = vals[a] + vals[b]
            best = s if best is None else jnp.maximum(best, s)
    return best


def _outproj_kernel(ya_ref, yb_ref, yc_ref, w_ref, x_ref, mod_ref, lng_ref, lnb_ref, wr_ref, rb_ref, tri_ref,
                    x1_ref, h2_ref, route_ref, cnt_ref, carry_sc, mix_sc, y_sc, hi_sc, lo_sc):
    i = pl.program_id(0)

    @pl.when(i == 0)
    def _init():
        carry_sc[...] = jnp.zeros_like(carry_sc)

    mix_sc[:, 0:A_Q] = ya_ref[...]
    mix_sc[:, A_Q:A_Q + B_V] = yb_ref[...]
    mix_sc[:, A_Q + B_V:D_MIX] = yc_ref[...]
    y_sc[...] = jnp.dot(mix_sc[...], w_ref[...], preferred_element_type=F32)

    def rows_step(r, carry):
        rows = pl.ds(pl.multiple_of(r * LN_ROWS, LN_ROWS), LN_ROWS)
        z = DEEPNORM_ALPHA * x_ref[rows, :] + mod_ref[0, 2:3, :] * y_sc[rows, :]
        x1 = _standardize(z) * lng_ref[...] + lnb_ref[...]
        x1_ref[rows, :] = x1
        h2 = x1 * (1.0 + mod_ref[0, 4:5, :]) + mod_ref[0, 3:4, :]
        h2_ref[rows, :] = h2
        hi = h2.astype(BF16)
        hi_sc[rows, :] = hi
        lo_sc[rows, :] = (h2 - hi.astype(F32)).astype(BF16)
        return carry

    lax.fori_loop(0, TM // LN_ROWS, rows_step, 0, unroll=2)

    lg = (jnp.dot(hi_sc[...], wr_ref[0], preferred_element_type=F32)
          + jnp.dot(hi_sc[...], wr_ref[1], preferred_element_type=F32)
          + jnp.dot(lo_sc[...], wr_ref[0], preferred_element_type=F32))
    logits = lg.T[0:N_EXPERTS, :]
    scores = jax.nn.sigmoid(logits)
    biased = scores + rb_ref[...]
    b_rows = [biased[e:e + 1, :] for e in range(N_EXPERTS)]
    s_rows = [scores[e:e + 1, :] for e in range(N_EXPERTS)]
    group_score = [_pair_max(b_rows[g * EXPERTS_PER_GROUP:(g + 1) * EXPERTS_PER_GROUP])
                   for g in range(N_EXPERT_GROUPS)]
    best = group_score[0]
    grp = jnp.zeros_like(best)
    for g in range(1, N_EXPERT_GROUPS):
        better = group_score[g] > best
        grp = jnp.where(better, float(g), grp)
        best = jnp.where(better, group_score[g], best)

    def pick(rows_, j):
        out = rows_[j]
        for g in range(1, N_EXPERT_GROUPS):
            out = jnp.where(grp == float(g), rows_[g * EXPERTS_PER_GROUP + j], out)
        return out

    vb = [pick(b_rows, j) for j in range(EXPERTS_PER_GROUP)]
    vs = [pick(s_rows, j) for j in range(EXPERTS_PER_GROUP)]
    chosen = []
    for j in range(EXPERTS_PER_GROUP):
        rank = jnp.zeros_like(best)
        for k in range(EXPERTS_PER_GROUP):
            if k == j:
                continue
            ahead = (vb[k] > vb[j]) | ((vb[k] == vb[j]) if k < j else False)
            rank = rank + jnp.where(ahead, 1.0, 0.0)
        chosen.append(rank < float(TOP_K))
    loc1 = jnp.full_like(best, float(EXPERTS_PER_GROUP))
    loc2 = jnp.full_like(best, -1.0)
    for j in range(EXPERTS_PER_GROUP):
        loc1 = jnp.where(chosen[j], jnp.minimum(loc1, float(j)), loc1)
        loc2 = jnp.where(chosen[j], jnp.maximum(loc2, float(j)), loc2)
    g1 = jnp.zeros_like(best)
    g2 = jnp.zeros_like(best)
    for j in range(EXPERTS_PER_GROUP):
        g1 = jnp.where(loc1 == float(j), vs[j], g1)
        g2 = jnp.where(loc2 == float(j), vs[j], g2)
    gsum = g1 + g2
    e1 = grp * float(EXPERTS_PER_GROUP) + loc1
    e2 = grp * float(EXPERTS_PER_GROUP) + loc2

    eidx = lax.broadcasted_iota(jnp.int32, (N_EXPERTS, TM), 0).astype(F32)
    is1 = eidx == e1
    is2 = eidx == e2
    sel = jnp.where(is1 | is2, 1.0, 0.0)
    before = jnp.dot(sel.astype(BF16), tri_ref[...], preferred_element_type=F32) + carry_sc[...]
    pos1 = jnp.sum(jnp.where(is1, before, 0.0), axis=0, keepdims=True)
    pos2 = jnp.sum(jnp.where(is2, before, 0.0), axis=0, keepdims=True)
    carry = carry_sc[...] + jnp.sum(sel, axis=1, keepdims=True)
    carry_sc[...] = carry
    cnt_ref[...] = carry[:, :LANES]

    route_ref[0:1, :] = e1
    route_ref[1:2, :] = e2
    route_ref[2:3, :] = pos1
    route_ref[3:4, :] = pos2
    route_ref[4:5, :] = g1 / gsum
    route_ref[5:6, :] = g2 / gsum
    route_ref[6:8, :] = jnp.zeros((2, TM), F32)


def _outproj_call(ya, yb, yc, w_out, xs, mod, ln_g, ln_b, wr_t, rb_rep, tri):
    row_spec = lambda w: pl.BlockSpec((TM, w), lambda i: (i, 0))
    vec_spec = pl.BlockSpec((1, D_MODEL), lambda i: (0, 0))
    return pl.pallas_call(
        _outproj_kernel,
        grid=(N_TILES,),
        in_specs=[
            row_spec(A_Q), row_spec(B_V), row_spec(C_U),
            pl.BlockSpec((D_MIX, D_MODEL), lambda i: (0, 0), pipeline_mode=pl.Buffered(1)),
            row_spec(D_MODEL),
            pl.BlockSpec((1, 6, D_MODEL), lambda i: (jnp.minimum(i, 1), 0, 0)),
            vec_spec, vec_spec,
            pl.BlockSpec((2, D_MODEL, LANES), lambda i: (0, 0, 0)),
            pl.BlockSpec((N_EXPERTS, TM), lambda i: (0, 0)),
            pl.BlockSpec((TM, TM), lambda i: (0, 0)),
        ],
        out_specs=[row_spec(D_MODEL), row_spec(D_MODEL),
                   pl.BlockSpec((8, TM), lambda i: (0, i)),
                   pl.BlockSpec((N_EXPERTS, LANES), lambda i: (0, 0))],
        out_shape=[jax.ShapeDtypeStruct((TOK, D_MODEL), F32), jax.ShapeDtypeStruct((TOK, D_MODEL), F32),
                   jax.ShapeDtypeStruct((8, TOK), F32), jax.ShapeDtypeStruct((N_EXPERTS, LANES), F32)],
        scratch_shapes=[pltpu.VMEM((N_EXPERTS, TM), F32), pltpu.VMEM((TM, D_MIX), BF16),
                        pltpu.VMEM((TM, D_MODEL), F32), pltpu.VMEM((TM, D_MODEL), BF16),
                        pltpu.VMEM((TM, D_MODEL), BF16)],
        compiler_params=pltpu.CompilerParams(dimension_semantics=("arbitrary",), vmem_limit_bytes=VMEM_LIMIT),
        name="out_proj",
    )(ya, yb, yc, w_out, xs, mod, ln_g, ln_b, wr_t, rb_rep, tri)


def _row_copy(src_ref, src_row, dst_ref, dst_row, sem):
    return pltpu.make_async_copy(src_ref.at[pl.ds(src_row, 1)], dst_ref.at[pl.ds(dst_row, 1)], sem)


def _slot_source_kernel(dest_ref, src_ref):
    def clear(s, carry):
        src_ref[s] = 0
        return carry

    def scatter(t, carry):
        src_ref[dest_ref[t]] = t
        src_ref[dest_ref[TOK + t]] = t
        return carry

    lax.fori_loop(0, MOE_ROWS, clear, 0, unroll=8)
    lax.fori_loop(0, TOK, scatter, 0, unroll=8)


def _slot_source_call(dest):
    return pl.pallas_call(
        _slot_source_kernel,
        in_specs=[pl.BlockSpec(memory_space=pltpu.SMEM)],
        out_specs=pl.BlockSpec(memory_space=pltpu.SMEM),
        out_shape=jax.ShapeDtypeStruct((MOE_ROWS,), jnp.int32),
        name="moe_slot_source",
    )(dest)


def _moe_kernel(be_ref, na_ref, src_ref, h_ref, wg_ref, wu_ref, wd_ref, o_ref, x_sc, sems):
    del be_ref
    b = pl.program_id(0)
    n_active = na_ref[0]

    def start_gather(block, slot):
        def body(r, carry):
            _row_copy(h_ref, src_ref[block * MOE_BM + r], x_sc.at[slot], r, sems.at[slot]).start()
            return carry
        lax.fori_loop(0, MOE_BM, body, 0, unroll=8)

    def wait_gather(slot):
        def body(r, carry):
            _row_copy(h_ref, 0, x_sc.at[slot], 0, sems.at[slot]).wait()
            return carry
        lax.fori_loop(0, MOE_BM, body, 0, unroll=8)

    @pl.when(b == 0)
    def _first():
        start_gather(0, 0)

    @pl.when(b + 1 < n_active)
    def _prefetch_next():
        start_gather(b + 1, (b + 1) % 2)

    @pl.when(b >= n_active)
    def _unused_block():
        o_ref[...] = jnp.zeros_like(o_ref)

    @pl.when(b < n_active)
    def _block():
        slot = b % 2
        wait_gather(slot)
        xb = x_sc[slot].astype(BF16)
        gate = jnp.dot(xb, wg_ref[0], preferred_element_type=F32)
        up = jnp.dot(xb, wu_ref[0], preferred_element_type=F32)
        act = (gate * jax.nn.sigmoid(gate) * up).astype(BF16)
        o_ref[...] = jnp.dot(act, wd_ref[0], preferred_element_type=F32)


def _moe_call(block_e, n_active, src, h2, w_gate, w_up, w_down):
    w_map = lambda b, be, na, sr: (be[b], 0, 0)
    return pl.pallas_call(
        _moe_kernel,
        grid_spec=pltpu.PrefetchScalarGridSpec(
            num_scalar_prefetch=3,
            grid=(MOE_NB,),
            in_specs=[pl.BlockSpec(memory_space=pl.ANY),
                      pl.BlockSpec((1, D_MODEL, D_EXPERT), w_map),
                      pl.BlockSpec((1, D_MODEL, D_EXPERT), w_map),
                      pl.BlockSpec((1, D_EXPERT, D_MODEL), w_map)],
            out_specs=pl.BlockSpec((MOE_BM, D_MODEL), lambda b, be, na, sr: (b, 0)),
            scratch_shapes=[pltpu.VMEM((2, MOE_BM, D_MODEL), F32), pltpu.SemaphoreType.DMA((2,))],
        ),
        out_shape=jax.ShapeDtypeStruct((MOE_ROWS, D_MODEL), F32),
        compiler_params=pltpu.CompilerParams(dimension_semantics=("arbitrary",), vmem_limit_bytes=VMEM_LIMIT),
        name="moe_experts",
    )(block_e, n_active, src, h2, w_gate, w_up, w_down)


def _combine_kernel(dest_ref, ys_ref, x1_ref, g1_ref, g2_ref, mod_ref, lng_ref, lnb_ref, o_ref, rows_sc, sem):
    base = pl.program_id(0) * TM

    def start(t, carry):
        tok = base + t
        _row_copy(ys_ref, dest_ref[tok], rows_sc.at[0], t, sem).start()
        _row_copy(ys_ref, dest_ref[TOK + tok], rows_sc.at[1], t, sem).start()
        return carry

    def wait(t, carry):
        _row_copy(ys_ref, 0, rows_sc.at[0], 0, sem).wait()
        _row_copy(ys_ref, 0, rows_sc.at[1], 0, sem).wait()
        return carry

    lax.fori_loop(0, TM, start, 0)
    lax.fori_loop(0, TM, wait, 0)
    y = g1_ref[...] * rows_sc[0] + g2_ref[...] * rows_sc[1]
    z = DEEPNORM_ALPHA * x1_ref[...] + mod_ref[0, 5:6, :] * y
    o_ref[...] = _standardize(z) * lng_ref[...] + lnb_ref[...]


def _combine_call(dest, ys, x1, g1, g2, mod, ln_g, ln_b):
    row_spec = lambda w: pl.BlockSpec((TM, w), lambda i, d: (i, 0))
    vec_spec = pl.BlockSpec((1, D_MODEL), lambda i, d: (0, 0))
    return pl.pallas_call(
        _combine_kernel,
        grid_spec=pltpu.PrefetchScalarGridSpec(
            num_scalar_prefetch=1,
            grid=(N_TILES,),
            in_specs=[pl.BlockSpec(memory_space=pl.ANY), row_spec(D_MODEL), row_spec(1), row_spec(1),
                      pl.BlockSpec((1, 6, D_MODEL), lambda i, d: (jnp.minimum(i, 1), 0, 0)),
                      vec_spec, vec_spec],
            out_specs=row_spec(D_MODEL),
            scratch_shapes=[pltpu.VMEM((2, TM, D_MODEL), F32), pltpu.SemaphoreType.DMA(())],
        ),
        out_shape=jax.ShapeDtypeStruct((TOK, D_MODEL), F32),
        compiler_params=pltpu.CompilerParams(dimension_semantics=("arbitrary",), vmem_limit_bytes=VMEM_LIMIT),
        name="moe_combine",
    )(dest, ys, x1, g1, g2, mod, ln_g, ln_b)


def _rope_tables(dim):
    n_freq = dim // 4
    inv = ROPE_THETA ** (-jnp.arange(n_freq, dtype=F32) / n_freq)
    t = jnp.arange(SEQ)
    rows = (t // GRID_W).astype(F32)
    cols = (t % GRID_W).astype(F32)
    lane = jnp.arange(LANES)
    within = lane % dim
    use_col = within >= dim // 2
    freq = within % n_freq
    second_half = (within % (dim // 2)) >= n_freq
    pos = jnp.where(use_col[None, :], cols[:, None], rows[:, None])
    ang = pos * inv[freq][None, :]
    cos = jnp.cos(ang)
    sin = jnp.where(second_half[None, :], jnp.sin(ang), -jnp.sin(ang))
    cos = jnp.concatenate([jnp.ones((CTX, LANES), F32), cos], axis=0)
    sin = jnp.concatenate([jnp.zeros((CTX, LANES), F32), sin], axis=0)
    return cos, sin


def kernel(x, c, ctx, c_ctx, w_ada, b_ada, w_in, w_out, a_q_norm, a_k_norm, a_out_norm, b_lambda, b_out_norm,
           c_spatial, c_spatial_bias, c_out_norm, ln1_g, ln1_b, ln2_g, ln2_b, w_router, router_bias,
           w_gate, w_up, w_down):
    assert x.shape == (1, SEQ, D_MODEL) and ctx.shape == (1, CTX, D_MODEL)
    cos_a, sin_a = _rope_tables(HEAD_DIM)
    cos_b, sin_b = _rope_tables(B_QK_DIM)
    tabs = (cos_a, sin_a, cos_b, sin_b)

    c_rep = jnp.broadcast_to(jnp.stack([c_ctx, c[0]])[:, :, None], (2, D_MODEL, LANES))
    mods = _ada_call(c_rep, w_ada, b_ada).reshape(DEPTH, 2, 6, D_MODEL)

    w_in_b = w_in.astype(BF16)
    cols = lambda off, n: w_in_b[:, :, off:off + n]
    w_rows = jnp.concatenate([cols(OFF_AK, A_KV), cols(OFF_BK, B_QK), cols(OFF_CU, C_U), cols(OFF_CV, C_U)], axis=2)
    w_t = jnp.swapaxes(jnp.concatenate([cols(OFF_AQ, A_Q), cols(OFF_BQ, B_QK), cols(OFF_AV, A_KV),
                                        cols(OFF_BV, B_V)], axis=2), 1, 2)
    tabs_t = tuple(t.T for t in tabs)
    col_rep = lambda a: jnp.broadcast_to(a[:, :, None], (DEPTH, HEAD_DIM, TM))
    gq_t, ga_t, gb_t = col_rep(a_q_norm), col_rep(a_out_norm), col_rep(b_out_norm)
    w_out_b = w_out.astype(BF16)
    ws_b = c_spatial.astype(BF16)
    bs_rep = jnp.broadcast_to(c_spatial_bias[:, :, :, None], (DEPTH, C_GROUPS, CHUNK, LANES))
    w_gate_b = w_gate.astype(BF16)
    w_up_b = w_up.astype(BF16)
    w_down_b = w_down.astype(BF16)
    wr_pad = jnp.pad(w_router, ((0, 0), (0, LANES - N_EXPERTS)))
    wr_hi = wr_pad.astype(BF16)
    wr_t = jnp.stack([wr_hi, (wr_pad - wr_hi.astype(F32)).astype(BF16)])
    rb_rep = jnp.broadcast_to(router_bias[:, None], (N_EXPERTS, TM))
    tri = jnp.triu(jnp.ones((TM, TM), BF16), k=1)

    xs = jnp.concatenate([ctx[0], x[0]], axis=0)
    for l in range(DEPTH):
        lam_init = 0.8 - 0.6 * math.exp(-0.3 * l)
        mod = mods[l]
        vec = lambda a: a[l].reshape(1, -1)
        qar, qaf, ka, va, qbr, qbf, kb, vb, yc = _inproj_call(
            xs, mod, w_rows[l], w_t[l], tabs, tabs_t, gq_t[l], vec(a_k_norm), ws_b[l], bs_rep[l], vec(c_out_norm))
        ya = _attn_call(qar, qaf, ka, va, ga_t[l], None, groups=A_GROUP, kv_heads=1, diff=False,
                        lam_init=lam_init, name="attn_gqa")
        yb = _attn_call(qbr, qbf, kb, vb, gb_t[l], b_lambda[l], groups=4, kv_heads=2, diff=True,
                        lam_init=lam_init, name="attn_diff")
        x1, h2, route, counts = _outproj_call(ya, yb, yc, w_out_b[l], xs, mod, vec(ln1_g), vec(ln1_b),
                                              wr_t, rb_rep, tri)

        cnt = counts[:, 0].astype(jnp.int32)
        padded = (cnt + MOE_BM - 1) // MOE_BM * MOE_BM
        ends = jnp.cumsum(padded)
        starts = ends - padded
        e1 = route[0].astype(jnp.int32)
        e2 = route[1].astype(jnp.int32)
        dest = jnp.concatenate([starts[e1] + route[2].astype(jnp.int32), starts[e2] + route[3].astype(jnp.int32)])
        n_active = (ends[-1] // MOE_BM).astype(jnp.int32)
        blk = jnp.minimum(jnp.arange(MOE_NB, dtype=jnp.int32), n_active - 1)
        block_e = jnp.minimum(jnp.sum(ends[None, :] <= (blk * MOE_BM)[:, None], axis=1), N_EXPERTS - 1).astype(jnp.int32)

        src = _slot_source_call(dest)
        ys = _moe_call(block_e, n_active.reshape(1), src, h2, w_gate_b[l], w_up_b[l], w_down_b[l])
        xs = _combine_call(dest, ys, x1, route[4].reshape(TOK, 1), route[5].reshape(TOK, 1), mod,
                           vec(ln2_g), vec(ln2_b))
    return xs[CTX:].reshape(1, SEQ, D_MODEL)
```

```python
import functools
import math

import jax
import jax.numpy as jnp
from jax import lax
from jax.experimental import pallas as pl
from jax.experimental.pallas import tpu as pltpu

F32 = jnp.float32
BF16 = jnp.bfloat16

D_MODEL = 2048
SEQ = 8192
CTX = 256
TOK = CTX + SEQ
DEPTH = 4
GRID_W = 64

HEAD_DIM = 128
A_HEADS = 8
A_KV_HEADS = 2
A_GROUP = A_HEADS // A_KV_HEADS
B_HEADS = 4
B_QK_DIM = 64
C_GROUPS = 4
CHUNK = 128
ROPE_THETA = 10000.0

A_Q = A_HEADS * HEAD_DIM
A_KV = A_KV_HEADS * HEAD_DIM
B_QK = B_HEADS * HEAD_DIM
B_V = B_HEADS * HEAD_DIM
C_U = C_GROUPS * HEAD_DIM
D_IN = A_Q + 2 * A_KV + 2 * B_QK + B_V + 2 * C_U
D_MIX = A_Q + B_V + C_U
OFF_AQ = 0
OFF_AK = OFF_AQ + A_Q
OFF_AV = OFF_AK + A_KV
OFF_BQ = OFF_AV + A_KV
OFF_BK = OFF_BQ + B_QK
OFF_BV = OFF_BK + B_QK
OFF_CU = OFF_BV + B_V
OFF_CV = OFF_CU + C_U

N_EXPERTS = 16
N_EXPERT_GROUPS = 4
EXPERTS_PER_GROUP = 4
TOP_K = 2
D_EXPERT = 1024

DEEPNORM_ALPHA = (2 * DEPTH) ** 0.25
NORM_EPS = 1e-6
LOG2_E = math.log2(math.e)

LANES = 128
VMEM_LIMIT = 60 * 1024 * 1024

TM = 256
N_TILES = TOK // TM
ADA_TN = 512
LN_ROWS = 16
ATT_CK = 512
ATT_UNROLL = 2
ATT_TQ = 256
MOE_BM = 256
MOE_NB = (TOK * TOP_K + N_EXPERTS * (MOE_BM - 1) + MOE_BM - 1) // MOE_BM
MOE_ROWS = MOE_NB * MOE_BM

NT_DIMS = (((1,), (1,)), ((), ()))


def _rms(x, g):
    return x * lax.rsqrt(jnp.mean(x * x, axis=-1, keepdims=True) + NORM_EPS) * g


def _gelu(x):
    return 0.5 * x * (1.0 + lax.erf(x * (2.0 ** -0.5)))


def _standardize(x):
    mu = jnp.mean(x, axis=-1, keepdims=True)
    xc = x - mu
    var = jnp.mean(xc * xc, axis=-1, keepdims=True)
    return xc * lax.rsqrt(var + NORM_EPS)


def _ada_kernel(c_ref, w_ref, b_ref, o_ref):
    for r in range(2):
        cv = c_ref[r]
        act = cv * jax.nn.sigmoid(cv)
        for j in range(ADA_TN // LANES):
            cols = slice(j * LANES, (j + 1) * LANES)
            o_ref[0, r:r + 1, cols] = jnp.sum(w_ref[0, :, cols] * act, axis=0, keepdims=True) + b_ref[0, :, cols]


def _ada_call(c_rep, w_ada, b_ada):
    n_out = w_ada.shape[-1]
    return pl.pallas_call(
        _ada_kernel,
        grid=(DEPTH, n_out // ADA_TN),
        in_specs=[
            pl.BlockSpec((2, D_MODEL, LANES), lambda l, n: (0, 0, 0)),
            pl.BlockSpec((1, D_MODEL, ADA_TN), lambda l, n: (l, 0, n)),
            pl.BlockSpec((1, 1, ADA_TN), lambda l, n: (l, 0, n)),
        ],
        out_specs=pl.BlockSpec((1, 2, ADA_TN), lambda l, n: (l, 0, n)),
        out_shape=jax.ShapeDtypeStruct((DEPTH, 2, n_out), F32),
        compiler_params=pltpu.CompilerParams(dimension_semantics=("arbitrary", "arbitrary"),
                                             vmem_limit_bytes=VMEM_LIMIT),
        name="ada_ln",
    )(c_rep, w_ada, b_ada.reshape(DEPTH, 1, n_out))


def _rope(x, cos, sin_signed, half, lane):
    fwd = pltpu.roll(x, LANES - half, 1)
    bwd = pltpu.roll(x, half, 1)
    partner = jnp.where((lane & (2 * half - 1)) < half, fwd, bwd)
    return x * cos + partner * sin_signed


def _rope_t(x, cos, sin_signed, half):
    blocks = [x[b * half:(b + 1) * half] for b in range(HEAD_DIM // half)]
    partner = jnp.concatenate([blocks[b ^ 1] for b in range(len(blocks))], axis=0)
    return x * cos + partner * sin_signed


T_QA = 0
T_QB = T_QA + A_Q
T_VA = T_QB + B_QK
T_VB = T_VA + A_KV
T_ROWS = T_VB + B_V
R_KA, R_KB, R_CU, R_CV = OFF_AK, OFF_BK, OFF_CU, OFF_CV
T_SEG = 512
WT_BLK = 256
WT_SRC_BLOCKS = tuple(c // WT_BLK for off, n in ((OFF_AQ, A_Q), (OFF_BQ, B_QK), (OFF_AV, A_KV), (OFF_BV, B_V))
                      for c in range(off, off + n, WT_BLK))


def _wt_kernel(blk_ref, w_ref, o_ref):
    del blk_ref
    o_ref[0] = w_ref[0].T.astype(BF16)


def _wt_call(w_in):
    blocks = jnp.asarray(WT_SRC_BLOCKS, jnp.int32)
    return pl.pallas_call(
        _wt_kernel,
        grid_spec=pltpu.PrefetchScalarGridSpec(
            num_scalar_prefetch=1,
            grid=(DEPTH, len(WT_SRC_BLOCKS)),
            in_specs=[pl.BlockSpec((1, D_MODEL, WT_BLK), lambda l, j, blk: (l, 0, blk[j]))],
            out_specs=pl.BlockSpec((1, WT_BLK, D_MODEL), lambda l, j, blk: (l, j, 0)),
        ),
        out_shape=jax.ShapeDtypeStruct((DEPTH, T_ROWS, D_MODEL), BF16),
        compiler_params=pltpu.CompilerParams(dimension_semantics=("arbitrary", "arbitrary")),
        name="w_in_transpose",
    )(blocks, w_in)


def _inproj_kernel(x_ref, mod_ref, wr_ref, wt_ref, cosa_ref, sina_ref, cosb_ref, sinb_ref,
                   cosat_ref, sinat_ref, cosbt_ref, sinbt_ref, gqt_ref, gk_ref, ws_ref, bs_ref, gc_ref,
                   qar_ref, qaf_ref, ka_ref, va_ref, qbr_ref, qbf_ref, kb_ref, vb_ref, yc_ref, pt_sc):
    x = x_ref[...]
    h = (x * (1.0 + mod_ref[0, 1:2, :]) + mod_ref[0, 0:1, :]).astype(BF16)
    lane = lax.broadcasted_iota(jnp.int32, (TM, LANES), 1)
    cosa, sina = cosa_ref[...], sina_ref[...]
    cosb, sinb = cosb_ref[...], sinb_ref[...]
    scale_a = HEAD_DIM ** -0.5 * LOG2_E
    scale_b = B_QK_DIM ** -0.5 * LOG2_E

    for r0 in range(0, T_ROWS, T_SEG):
        n = min(T_SEG, T_ROWS - r0)
        pt_sc[r0:r0 + n, :] = lax.dot_general(wt_ref[r0:r0 + n, :], h, NT_DIMS, preferred_element_type=F32)
    for j in range(A_HEADS):
        q = pt_sc[T_QA + j * HEAD_DIM:T_QA + (j + 1) * HEAD_DIM, :]
        qn = q * lax.rsqrt(jnp.mean(q * q, axis=0, keepdims=True) + NORM_EPS) * (gqt_ref[...] * scale_a)
        qaf_ref[j] = qn.astype(BF16)
        qar_ref[j] = _rope_t(qn, cosat_ref[...], sinat_ref[...], 32).astype(BF16)
    zeros_half = jnp.zeros((B_QK_DIM, TM), BF16)
    for j in range(B_HEADS):
        qs = pt_sc[T_QB + j * HEAD_DIM:T_QB + (j + 1) * HEAD_DIM, :] * scale_b
        for q, dst in ((qs, qbf_ref), (_rope_t(qs, cosbt_ref[...], sinbt_ref[...], 16), qbr_ref)):
            qb = q.astype(BF16)
            dst[2 * j] = jnp.concatenate([qb[:B_QK_DIM], zeros_half], axis=0)
            dst[2 * j + 1] = jnp.concatenate([zeros_half, qb[B_QK_DIM:]], axis=0)
    for j in range(A_KV_HEADS):
        va_ref[j] = pt_sc[T_VA + j * HEAD_DIM:T_VA + (j + 1) * HEAD_DIM, :].astype(BF16)
    for j in range(B_HEADS):
        vb_ref[j] = pt_sc[T_VB + j * HEAD_DIM:T_VB + (j + 1) * HEAD_DIM, :].astype(BF16)

    def proj(col):
        p = jnp.dot(h, wr_ref[:, col:col + 2 * LANES], preferred_element_type=F32)
        return p[:, :LANES], p[:, LANES:]

    for j, k in enumerate(proj(R_KA)):
        ka_ref[j] = _rope(_rms(k, gk_ref[...]), cosa, sina, 32, lane).astype(BF16)
    for j2 in range(B_HEADS // 2):
        for j, k in zip((2 * j2, 2 * j2 + 1), proj(R_KB + j2 * 2 * LANES)):
            kb_ref[j] = _rope(k, cosb, sinb, 16, lane).astype(BF16)
    for j2 in range(C_GROUPS // 2):
        us = proj(R_CU + j2 * 2 * LANES)
        vs = proj(R_CV + j2 * 2 * LANES)
        for g, u, v in zip((2 * j2, 2 * j2 + 1), us, vs):
            u = _gelu(u)
            v = _standardize(_gelu(v)).astype(BF16)
            for c in range(TM // CHUNK):
                rows = slice(c * CHUNK, (c + 1) * CHUNK)
                mixed = jnp.dot(ws_ref[g], v[rows], preferred_element_type=F32) + bs_ref[g]
                yc_ref[rows, g * LANES:(g + 1) * LANES] = _rms(u[rows] * mixed, gc_ref[...]).astype(BF16)


def _inproj_call(xs, mod, w_rows, w_t, tabs, tabs_t, gq_t, gk, ws, bs, gc):
    def heads(n):
        return (jax.ShapeDtypeStruct((n, TOK, HEAD_DIM), BF16),
                pl.BlockSpec((n, TM, HEAD_DIM), lambda i: (0, i, 0)))

    def heads_t(n):
        return (jax.ShapeDtypeStruct((n, HEAD_DIM, TOK), BF16),
                pl.BlockSpec((n, HEAD_DIM, TM), lambda i: (0, 0, i)))

    outs = [heads_t(A_HEADS), heads_t(A_HEADS), heads(A_KV_HEADS), heads_t(A_KV_HEADS),
            heads_t(2 * B_HEADS), heads_t(2 * B_HEADS), heads(B_HEADS), heads_t(B_HEADS),
            (jax.ShapeDtypeStruct((TOK, C_U), BF16), pl.BlockSpec((TM, C_U), lambda i: (i, 0)))]
    tab_spec = pl.BlockSpec((TM, LANES), lambda i: (i, 0))
    tab_t_spec = pl.BlockSpec((HEAD_DIM, TM), lambda i: (0, i))
    vec_spec = pl.BlockSpec((1, LANES), lambda i: (0, 0))
    return pl.pallas_call(
        _inproj_kernel,
        grid=(N_TILES,),
        in_specs=[
            pl.BlockSpec((TM, D_MODEL), lambda i: (i, 0)),
            pl.BlockSpec((1, 6, D_MODEL), lambda i: (jnp.minimum(i, 1), 0, 0)),
            pl.BlockSpec((D_MODEL, D_IN), lambda i: (0, 0), pipeline_mode=pl.Buffered(1)),
            pl.BlockSpec((T_ROWS, D_MODEL), lambda i: (0, 0), pipeline_mode=pl.Buffered(1)),
            tab_spec, tab_spec, tab_spec, tab_spec,
            tab_t_spec, tab_t_spec, tab_t_spec, tab_t_spec,
            pl.BlockSpec((HEAD_DIM, TM), lambda i: (0, 0)),
            vec_spec,
            pl.BlockSpec((C_GROUPS, CHUNK, CHUNK), lambda i: (0, 0, 0)),
            pl.BlockSpec((C_GROUPS, CHUNK, LANES), lambda i: (0, 0, 0)),
            vec_spec,
        ],
        out_specs=[o[1] for o in outs],
        out_shape=[o[0] for o in outs],
        scratch_shapes=[pltpu.VMEM((T_ROWS, TM), F32)],
        compiler_params=pltpu.CompilerParams(dimension_semantics=("arbitrary",), vmem_limit_bytes=VMEM_LIMIT),
        name="in_proj",
    )(xs, mod, w_rows, w_t, *tabs, *tabs_t, gq_t, gk, ws, bs, gc)


def _attn_kernel(*refs, groups, kv_heads, diff, lam_init):
    if diff:
        qr_ref, qf_ref, k_ref, vt_ref, lam_ref, g_ref, o_ref, m_sc, l_sc, acc_sc, s_sc, p_sc, a_sc, c_sc = refs
    else:
        qr_ref, qf_ref, k_ref, vt_ref, g_ref, o_ref, m_sc, l_sc, acc_sc, s_sc, p_sc, a_sc, c_sc = refs
    i = pl.program_id(1)
    n_chunks = SEQ // ATT_CK
    kv_of = lambda g: g // (groups // kv_heads)

    for g in range(groups):
        kc = k_ref[kv_of(g), 0:CTX, :]
        vc = vt_ref[kv_of(g), :, 0:CTX]
        s = jnp.dot(kc, qf_ref[g], preferred_element_type=F32)
        m0 = jnp.max(s, axis=0, keepdims=True)
        p = jnp.exp2(s - m0)
        m_sc[g] = m0
        l_sc[g] = jnp.sum(p, axis=0, keepdims=True)
        acc_sc[g] = jnp.dot(vc, p.astype(BF16), preferred_element_type=F32)

    def chunk_off(c):
        return pl.multiple_of(CTX + jnp.clip(c, 0, n_chunks - 1) * ATT_CK, CTX)

    def scores(g, c, slot):
        s = jnp.dot(k_ref[kv_of(g), pl.ds(chunk_off(c), ATT_CK), :], qr_ref[g], preferred_element_type=F32)
        s_sc[2 * g + slot] = s
        c_sc[2 * g + slot] = jnp.max(s, axis=0, keepdims=True)

    def softmax(g, slot):
        m_prev = m_sc[g]
        m_new = jnp.maximum(m_prev, c_sc[2 * g + slot])
        alpha = jnp.exp2(m_prev - m_new)
        p = jnp.exp2(s_sc[2 * g + slot] - m_new)
        l_sc[g] = alpha * l_sc[g] + jnp.sum(p, axis=0, keepdims=True)
        m_sc[g] = m_new
        a_sc[2 * g + slot] = alpha
        p_sc[2 * g + slot] = p.astype(BF16)

    def values(g, c, slot):
        acc_sc[g] = a_sc[2 * g + slot] * acc_sc[g] + jnp.dot(vt_ref[kv_of(g), :, pl.ds(chunk_off(c), ATT_CK)],
                                                             p_sc[2 * g + slot], preferred_element_type=F32)

    @pl.when(i >= CTX // ATT_TQ)
    def _latent_keys():
        for g in range(groups):
            scores(g, 0, 0)
            a_sc[2 * g + 1] = jnp.ones((1, ATT_TQ), F32)
            p_sc[2 * g + 1] = jnp.zeros((ATT_CK, ATT_TQ), BF16)

        def body(t, carry):
            for u in range(ATT_UNROLL):
                c0 = 2 * (ATT_UNROLL * t + u)
                for g in range(groups):
                    scores(g, c0 + 1, 1)
                    softmax(g, 0)
                    values(g, c0 - 1, 1)
                for g in range(groups):
                    scores(g, c0 + 2, 0)
                    softmax(g, 1)
                    values(g, c0, 0)
            return carry

        lax.fori_loop(0, n_chunks // (2 * ATT_UNROLL), body, 0)
        for g in range(groups):
            values(g, n_chunks - 1, 1)

    def finish(o_t, gain):
        o_t = o_t * lax.rsqrt(jnp.mean(o_t * o_t, axis=0, keepdims=True) + NORM_EPS) * gain
        return o_t.T.astype(BF16)

    if diff:
        lam = lam_ref[...]
        lam_val = (jnp.exp(jnp.sum(lam[0:1] * lam[1:2], axis=-1, keepdims=True))
                   - jnp.exp(jnp.sum(lam[2:3] * lam[3:4], axis=-1, keepdims=True)) + lam_init)
        for h in range(kv_heads):
            d = acc_sc[2 * h] / l_sc[2 * h] - lam_val * (acc_sc[2 * h + 1] / l_sc[2 * h + 1])
            o_ref[:, h * HEAD_DIM:(h + 1) * HEAD_DIM] = finish(d, g_ref[...] * (1.0 - lam_init))
    else:
        for g in range(groups):
            o_ref[:, g * HEAD_DIM:(g + 1) * HEAD_DIM] = finish(acc_sc[g] / l_sc[g], g_ref[...])


def _attn_call(qr, qf, k, vt, g_out_t, lam, *, groups, kv_heads, diff, lam_init, name):
    n_steps = k.shape[0] // kv_heads
    k_spec = pl.BlockSpec((kv_heads, TOK, HEAD_DIM), lambda h, i: (h, 0, 0))
    vt_spec = pl.BlockSpec((kv_heads, HEAD_DIM, TOK), lambda h, i: (h, 0, 0))
    q_spec = pl.BlockSpec((groups, HEAD_DIM, ATT_TQ), lambda h, i: (h, 0, i))
    in_specs = [q_spec, q_spec, k_spec, vt_spec]
    args = [qr, qf, k, vt]
    if diff:
        in_specs.append(pl.BlockSpec((4, B_QK_DIM), lambda h, i: (0, 0)))
        args.append(lam)
        out_w = kv_heads * HEAD_DIM
    else:
        out_w = groups * HEAD_DIM
    in_specs.append(pl.BlockSpec((HEAD_DIM, ATT_TQ), lambda h, i: (0, 0)))
    args.append(g_out_t)
    return pl.pallas_call(
        functools.partial(_attn_kernel, groups=groups, kv_heads=kv_heads, diff=diff, lam_init=lam_init),
        grid=(n_steps, TOK // ATT_TQ),
        in_specs=in_specs,
        out_specs=pl.BlockSpec((ATT_TQ, out_w), lambda h, i: (i, h)),
        out_shape=jax.ShapeDtypeStruct((TOK, n_steps * out_w), BF16),
        scratch_shapes=[pltpu.VMEM((groups, 1, ATT_TQ), F32), pltpu.VMEM((groups, 1, ATT_TQ), F32),
                        pltpu.VMEM((groups, HEAD_DIM, ATT_TQ), F32),
                        pltpu.VMEM((2 * groups, ATT_CK, ATT_TQ), F32),
                        pltpu.VMEM((2 * groups, ATT_CK, ATT_TQ), BF16),
                        pltpu.VMEM((2 * groups, 1, ATT_TQ), F32),
                        pltpu.VMEM((2 * groups, 1, ATT_TQ), F32)],
        compiler_params=pltpu.CompilerParams(dimension_semantics=("arbitrary", "arbitrary"),
                                             vmem_limit_bytes=VMEM_LIMIT),
        name=name,
    )(*args)


def _pair_max(vals):
    best = None
    for a in range(len(vals)):
        for b in range(a + 1, len(vals)):
            s = vals[a] + vals[b]
            best = s if best is None else jnp.maximum(best, s)
    return best


def _outproj_kernel(ya_ref, yb_ref, yc_ref, w_ref, x_ref, mod_ref, lng_ref, lnb_ref, wr_ref, rb_ref, tri_ref,
                    x1_ref, h2_ref, route_ref, cnt_ref, carry_sc, mix_sc, y_sc, hi_sc, lo_sc):
    i = pl.program_id(0)

    @pl.when(i == 0)
    def _init():
        carry_sc[...] = jnp.zeros_like(carry_sc)

    mix_sc[:, 0:A_Q] = ya_ref[...]
    mix_sc[:, A_Q:A_Q + B_V] = yb_ref[...]
    mix_sc[:, A_Q + B_V:D_MIX] = yc_ref[...]
    y_sc[...] = jnp.dot(mix_sc[...], w_ref[...], preferred_element_type=F32)

    def rows_step(r, carry):
        rows = pl.ds(pl.multiple_of(r * LN_ROWS, LN_ROWS), LN_ROWS)
        z = DEEPNORM_ALPHA * x_ref[rows, :] + mod_ref[0, 2:3, :] * y_sc[rows, :]
        x1 = _standardize(z) * lng_ref[...] + lnb_ref[...]
        x1_ref[rows, :] = x1
        h2 = x1 * (1.0 + mod_ref[0, 4:5, :]) + mod_ref[0, 3:4, :]
        h2_ref[rows, :] = h2
        hi = h2.astype(BF16)
        hi_sc[rows, :] = hi
        lo_sc[rows, :] = (h2 - hi.astype(F32)).astype(BF16)
        return carry

    lax.fori_loop(0, TM // LN_ROWS, rows_step, 0, unroll=2)

    lg = (jnp.dot(hi_sc[...], wr_ref[0], preferred_element_type=F32)
          + jnp.dot(hi_sc[...], wr_ref[1], preferred_element_type=F32)
          + jnp.dot(lo_sc[...], wr_ref[0], preferred_element_type=F32))
    logits = lg.T[0:N_EXPERTS, :]
    scores = jax.nn.sigmoid(logits)
    biased = scores + rb_ref[...]
    b_rows = [biased[e:e + 1, :] for e in range(N_EXPERTS)]
    s_rows = [scores[e:e + 1, :] for e in range(N_EXPERTS)]
    group_score = [_pair_max(b_rows[g * EXPERTS_PER_GROUP:(g + 1) * EXPERTS_PER_GROUP])
                   for g in range(N_EXPERT_GROUPS)]
    best = group_score[0]
    grp = jnp.zeros_like(best)
    for g in range(1, N_EXPERT_GROUPS):
        better = group_score[g] > best
        grp = jnp.where(better, float(g), grp)
        best = jnp.where(better, group_score[g], best)

    def pick(rows_, j):
        out = rows_[j]
        for g in range(1, N_EXPERT_GROUPS):
            out = jnp.where(grp == float(g), rows_[g * EXPERTS_PER_GROUP + j], out)
        return out

    vb = [pick(b_rows, j) for j in range(EXPERTS_PER_GROUP)]
    vs = [pick(s_rows, j) for j in range(EXPERTS_PER_GROUP)]
    chosen = []
    for j in range(EXPERTS_PER_GROUP):
        rank = jnp.zeros_like(best)
        for k in range(EXPERTS_PER_GROUP):
            if k == j:
                continue
            ahead = (vb[k] > vb[j]) | ((vb[k] == vb[j]) if k < j else False)
            rank = rank + jnp.where(ahead, 1.0, 0.0)
        chosen.append(rank < float(TOP_K))
    loc1 = jnp.full_like(best, float(EXPERTS_PER_GROUP))
    loc2 = jnp.full_like(best, -1.0)
    for j in range(EXPERTS_PER_GROUP):
        loc1 = jnp.where(chosen[j], jnp.minimum(loc1, float(j)), loc1)
        loc2 = jnp.where(chosen[j], jnp.maximum(loc2, float(j)), loc2)
    g1 = jnp.zeros_like(best)
    g2 = jnp.zeros_like(best)
    for j in range(EXPERTS_PER_GROUP):
        g1 = jnp.where(loc1 == float(j), vs[j], g1)
        g2 = jnp.where(loc2 == float(j), vs[j], g2)
    gsum = g1 + g2
    e1 = grp * float(EXPERTS_PER_GROUP) + loc1
    e2 = grp * float(EXPERTS_PER_GROUP) + loc2

    eidx = lax.broadcasted_iota(jnp.int32, (N_EXPERTS, TM), 0).astype(F32)
    is1 = eidx == e1
    is2 = eidx == e2
    sel = jnp.where(is1 | is2, 1.0, 0.0)
    before = jnp.dot(sel.astype(BF16), tri_ref[...], preferred_element_type=F32) + carry_sc[...]
    pos1 = jnp.sum(jnp.where(is1, before, 0.0), axis=0, keepdims=True)
    pos2 = jnp.sum(jnp.where(is2, before, 0.0), axis=0, keepdims=True)
    carry = carry_sc[...] + jnp.sum(sel, axis=1, keepdims=True)
    carry_sc[...] = carry
    cnt_ref[...] = carry[:, :LANES]

    route_ref[0:1, :] = e1
    route_ref[1:2, :] = e2
    route_ref[2:3, :] = pos1
    route_ref[3:4, :] = pos2
    route_ref[4:5, :] = g1 / gsum
    route_ref[5:6, :] = g2 / gsum
    route_ref[6:8, :] = jnp.zeros((2, TM), F32)


def _outproj_call(ya, yb, yc, w_out, xs, mod, ln_g, ln_b, wr_t, rb_rep, tri):
    row_spec = lambda w: pl.BlockSpec((TM, w), lambda i: (i, 0))
    vec_spec = pl.BlockSpec((1, D_MODEL), lambda i: (0, 0))
    return pl.pallas_call(
        _outproj_kernel,
        grid=(N_TILES,),
        in_specs=[
            row_spec(A_Q), row_spec(B_V), row_spec(C_U),
            pl.BlockSpec((D_MIX, D_MODEL), lambda i: (0, 0), pipeline_mode=pl.Buffered(1)),
            row_spec(D_MODEL),
            pl.BlockSpec((1, 6, D_MODEL), lambda i: (jnp.minimum(i, 1), 0, 0)),
            vec_spec, vec_spec,
            pl.BlockSpec((2, D_MODEL, LANES), lambda i: (0, 0, 0)),
            pl.BlockSpec((N_EXPERTS, TM), lambda i: (0, 0)),
            pl.BlockSpec((TM, TM), lambda i: (0, 0)),
        ],
        out_specs=[row_spec(D_MODEL), row_spec(D_MODEL),
                   pl.BlockSpec((8, TM), lambda i: (0, i)),
                   pl.BlockSpec((N_EXPERTS, LANES), lambda i: (0, 0))],
        out_shape=[jax.ShapeDtypeStruct((TOK, D_MODEL), F32), jax.ShapeDtypeStruct((TOK, D_MODEL), F32),
                   jax.ShapeDtypeStruct((8, TOK), F32), jax.ShapeDtypeStruct((N_EXPERTS, LANES), F32)],
        scratch_shapes=[pltpu.VMEM((N_EXPERTS, TM), F32), pltpu.VMEM((TM, D_MIX), BF16),
                        pltpu.VMEM((TM, D_MODEL), F32), pltpu.VMEM((TM, D_MODEL), BF16),
                        pltpu.VMEM((TM, D_MODEL), BF16)],
        compiler_params=pltpu.CompilerParams(dimension_semantics=("arbitrary",), vmem_limit_bytes=VMEM_LIMIT),
        name="out_proj",
    )(ya, yb, yc, w_out, xs, mod, ln_g, ln_b, wr_t, rb_rep, tri)


def _row_copy(src_ref, src_row, dst_ref, dst_row, sem):
    return pltpu.make_async_copy(src_ref.at[pl.ds(src_row, 1)], dst_ref.at[pl.ds(dst_row, 1)], sem)


def _slot_source_kernel(dest_ref, src_ref):
    def clear(s, carry):
        src_ref[s] = 0
        return carry

    def scatter(t, carry):
        src_ref[dest_ref[t]] = t
        src_ref[dest_ref[TOK + t]] = t
        return carry

    lax.fori_loop(0, MOE_ROWS, clear, 0, unroll=8)
    lax.fori_loop(0, TOK, scatter, 0, unroll=8)


def _slot_source_call(dest):
    return pl.pallas_call(
        _slot_source_kernel,
        in_specs=[pl.BlockSpec(memory_space=pltpu.SMEM)],
        out_specs=pl.BlockSpec(memory_space=pltpu.SMEM),
        out_shape=jax.ShapeDtypeStruct((MOE_ROWS,), jnp.int32),
        name="moe_slot_source",
    )(dest)


def _moe_kernel(be_ref, na_ref, src_ref, h_ref, wg_ref, wu_ref, wd_ref, o_ref, x_sc, sems):
    del be_ref
    b = pl.program_id(0)
    n_active = na_ref[0]

    def start_gather(block, slot):
        def body(r, carry):
            _row_copy(h_ref, src_ref[block * MOE_BM + r], x_sc.at[slot], r, sems.at[slot]).start()
            return carry
        lax.fori_loop(0, MOE_BM, body, 0, unroll=8)

    def wait_gather(slot):
        def body(r, carry):
            _row_copy(h_ref, 0, x_sc.at[slot], 0, sems.at[slot]).wait()
            return carry
        lax.fori_loop(0, MOE_BM, body, 0, unroll=8)

    @pl.when(b == 0)
    def _first():
        start_gather(0, 0)

    @pl.when(b + 1 < n_active)
    def _prefetch_next():
        start_gather(b + 1, (b + 1) % 2)

    @pl.when(b >= n_active)
    def _unused_block():
        o_ref[...] = jnp.zeros_like(o_ref)

    @pl.when(b < n_active)
    def _block():
        slot = b % 2
        wait_gather(slot)
        xb = x_sc[slot]
        gate = jnp.dot(xb, wg_ref[0], preferred_element_type=F32)
        up = jnp.dot(xb, wu_ref[0], preferred_element_type=F32)
        act = gate * jax.nn.sigmoid(gate) * up
        o_ref[...] = jnp.dot(act, wd_ref[0], preferred_element_type=F32)


def _moe_call(block_e, n_active, src, h2, w_gate, w_up, w_down):
    w_map = lambda b, be, na, sr: (be[b], 0, 0)
    return pl.pallas_call(
        _moe_kernel,
        grid_spec=pltpu.PrefetchScalarGridSpec(
            num_scalar_prefetch=3,
            grid=(MOE_NB,),
            in_specs=[pl.BlockSpec(memory_space=pl.ANY),
                      pl.BlockSpec((1, D_MODEL, D_EXPERT), w_map),
                      pl.BlockSpec((1, D_MODEL, D_EXPERT), w_map),
                      pl.BlockSpec((1, D_EXPERT, D_MODEL), w_map)],
            out_specs=pl.BlockSpec((MOE_BM, D_MODEL), lambda b, be, na, sr: (b, 0)),
            scratch_shapes=[pltpu.VMEM((2, MOE_BM, D_MODEL), F32), pltpu.SemaphoreType.DMA((2,))],
        ),
        out_shape=jax.ShapeDtypeStruct((MOE_ROWS, D_MODEL), F32),
        compiler_params=pltpu.CompilerParams(dimension_semantics=("arbitrary",), vmem_limit_bytes=VMEM_LIMIT),
        name="moe_experts",
    )(block_e, n_active, src, h2, w_gate, w_up, w_down)


def _combine_kernel(dest_ref, ys_ref, x1_ref, g1_ref, g2_ref, mod_ref, lng_ref, lnb_ref, o_ref, rows_sc, sem):
    base = pl.program_id(0) * TM

    def start(t, carry):
        tok = base + t
        _row_copy(ys_ref, dest_ref[tok], rows_sc.at[0], t, sem).start()
        _row_copy(ys_ref, dest_ref[TOK + tok], rows_sc.at[1], t, sem).start()
        return carry

    def wait(t, carry):
        _row_copy(ys_ref, 0, rows_sc.at[0], 0, sem).wait()
        _row_copy(ys_ref, 0, rows_sc.at[1], 0, sem).wait()
        return carry

    lax.fori_loop(0, TM, start, 0)
    lax.fori_loop(0, TM, wait, 0)
    y = g1_ref[...] * rows_sc[0] + g2_ref[...] * rows_sc[1]
    z = DEEPNORM_ALPHA * x1_ref[...] + mod_ref[0, 5:6, :] * y
    o_ref[...] = _standardize(z) * lng_ref[...] + lnb_ref[...]


def _combine_call(dest, ys, x1, g1, g2, mod, ln_g, ln_b):
    row_spec = lambda w: pl.BlockSpec((TM, w), lambda i, d: (i, 0))
    vec_spec = pl.BlockSpec((1, D_MODEL), lambda i, d: (0, 0))
    return pl.pallas_call(
        _combine_kernel,
        grid_spec=pltpu.PrefetchScalarGridSpec(
            num_scalar_prefetch=1,
            grid=(N_TILES,),
            in_specs=[pl.BlockSpec(memory_space=pl.ANY), row_spec(D_MODEL), row_spec(1), row_spec(1),
                      pl.BlockSpec((1, 6, D_MODEL), lambda i, d: (jnp.minimum(i, 1), 0, 0)),
                      vec_spec, vec_spec],
            out_specs=row_spec(D_MODEL),
            scratch_shapes=[pltpu.VMEM((2, TM, D_MODEL), F32), pltpu.SemaphoreType.DMA(())],
        ),
        out_shape=jax.ShapeDtypeStruct((TOK, D_MODEL), F32),
        compiler_params=pltpu.CompilerParams(dimension_semantics=("arbitrary",), vmem_limit_bytes=VMEM_LIMIT),
        name="moe_combine",
    )(dest, ys, x1, g1, g2, mod, ln_g, ln_b)


def _rope_tables(dim):
    n_freq = dim // 4
    inv = ROPE_THETA ** (-jnp.arange(n_freq, dtype=F32) / n_freq)
    t = jnp.arange(SEQ)
    rows = (t // GRID_W).astype(F32)
    cols = (t % GRID_W).astype(F32)
    lane = jnp.arange(LANES)
    within = lane % dim
    use_col = within >= dim // 2
    freq = within % n_freq
    second_half = (within % (dim // 2)) >= n_freq
    pos = jnp.where(use_col[None, :], cols[:, None], rows[:, None])
    ang = pos * inv[freq][None, :]
    cos = jnp.cos(ang)
    sin = jnp.where(second_half[None, :], jnp.sin(ang), -jnp.sin(ang))
    cos = jnp.concatenate([jnp.ones((CTX, LANES), F32), cos], axis=0)
    sin = jnp.concatenate([jnp.zeros((CTX, LANES), F32), sin], axis=0)
    return cos, sin


def kernel(x, c, ctx, c_ctx, w_ada, b_ada, w_in, w_out, a_q_norm, a_k_norm, a_out_norm, b_lambda, b_out_norm,
           c_spatial, c_spatial_bias, c_out_norm, ln1_g, ln1_b, ln2_g, ln2_b, w_router, router_bias,
           w_gate, w_up, w_down):
    assert x.shape == (1, SEQ, D_MODEL) and ctx.shape == (1, CTX, D_MODEL)
    cos_a, sin_a = _rope_tables(HEAD_DIM)
    cos_b, sin_b = _rope_tables(B_QK_DIM)
    tabs = (cos_a, sin_a, cos_b, sin_b)

    c_rep = jnp.broadcast_to(jnp.stack([c_ctx, c[0]])[:, :, None], (2, D_MODEL, LANES))
    mods = _ada_call(c_rep, w_ada, b_ada).reshape(DEPTH, 2, 6, D_MODEL)

    w_rows = w_in.astype(BF16)
    w_t = _wt_call(w_in)
    tabs_t = tuple(t.T for t in tabs)
    col_rep = lambda a: jnp.broadcast_to(a[:, :, None], (DEPTH, HEAD_DIM, TM))
    gq_t, ga_t, gb_t = col_rep(a_q_norm), col_rep(a_out_norm), col_rep(b_out_norm)
    w_out_b = w_out.astype(BF16)
    ws_b = c_spatial.astype(BF16)
    bs_rep = jnp.broadcast_to(c_spatial_bias[:, :, :, None], (DEPTH, C_GROUPS, CHUNK, LANES))
    wr_pad = jnp.pad(w_router, ((0, 0), (0, LANES - N_EXPERTS)))
    wr_hi = wr_pad.astype(BF16)
    wr_t = jnp.stack([wr_hi, (wr_pad - wr_hi.astype(F32)).astype(BF16)])
    rb_rep = jnp.broadcast_to(router_bias[:, None], (N_EXPERTS, TM))
    tri = jnp.triu(jnp.ones((TM, TM), BF16), k=1)

    xs = jnp.concatenate([ctx[0], x[0]], axis=0)
    for l in range(DEPTH):
        lam_init = 0.8 - 0.6 * math.exp(-0.3 * l)
        mod = mods[l]
        vec = lambda a: a[l].reshape(1, -1)
        qar, qaf, ka, va, qbr, qbf, kb, vb, yc = _inproj_call(
            xs, mod, w_rows[l], w_t[l], tabs, tabs_t, gq_t[l], vec(a_k_norm), ws_b[l], bs_rep[l], vec(c_out_norm))
        ya = _attn_call(qar, qaf, ka, va, ga_t[l], None, groups=A_GROUP, kv_heads=1, diff=False,
                        lam_init=lam_init, name="attn_gqa")
        yb = _attn_call(qbr, qbf, kb, vb, gb_t[l], b_lambda[l], groups=4, kv_heads=2, diff=True,
                        lam_init=lam_init, name="attn_diff")
        x1, h2, route, counts = _outproj_call(ya, yb, yc, w_out_b[l], xs, mod, vec(ln1_g), vec(ln1_b),
                                              wr_t, rb_rep, tri)

        cnt = counts[:, 0].astype(jnp.int32)
        padded = (cnt + MOE_BM - 1) // MOE_BM * MOE_BM
        ends = jnp.cumsum(padded)
        starts = ends - padded
        e1 = route[0].astype(jnp.int32)
        e2 = route[1].astype(jnp.int32)
        dest = jnp.concatenate([starts[e1] + route[2].astype(jnp.int32), starts[e2] + route[3].astype(jnp.int32)])
        n_active = (ends[-1] // MOE_BM).astype(jnp.int32)
        blk = jnp.minimum(jnp.arange(MOE_NB, dtype=jnp.int32), n_active - 1)
        block_e = jnp.minimum(jnp.sum(ends[None, :] <= (blk * MOE_BM)[:, None], axis=1), N_EXPERTS - 1).astype(jnp.int32)

        src = _slot_source_call(dest)
        ys = _moe_call(block_e, n_active.reshape(1), src, h2, w_gate[l], w_up[l], w_down[l])
        xs = _combine_call(dest, ys, x1, route[4].reshape(TOK, 1), route[5].reshape(TOK, 1), mod,
                           vec(ln2_g), vec(ln2_b))
    return xs[CTX:].reshape(1, SEQ, D_MODEL)
```

```python
import functools
import math

import jax
import jax.numpy as jnp
from jax import lax
from jax.experimental import pallas as pl
from jax.experimental.pallas import tpu as pltpu

F32 = jnp.float32
BF16 = jnp.bfloat16

D_MODEL = 2048
SEQ = 8192
CTX = 256
TOK = CTX + SEQ
DEPTH = 4
GRID_W = 64

HEAD_DIM = 128
A_HEADS = 8
A_KV_HEADS = 2
A_GROUP = A_HEADS // A_KV_HEADS
B_HEADS = 4
B_QK_DIM = 64
C_GROUPS = 4
CHUNK = 128
ROPE_THETA = 10000.0

A_Q = A_HEADS * HEAD_DIM
A_KV = A_KV_HEADS * HEAD_DIM
B_QK = B_HEADS * HEAD_DIM
B_V = B_HEADS * HEAD_DIM
C_U = C_GROUPS * HEAD_DIM
D_IN = A_Q + 2 * A_KV + 2 * B_QK + B_V + 2 * C_U
D_MIX = A_Q + B_V + C_U
OFF_AQ = 0
OFF_AK = OFF_AQ + A_Q
OFF_AV = OFF_AK + A_KV
OFF_BQ = OFF_AV + A_KV
OFF_BK = OFF_BQ + B_QK
OFF_BV = OFF_BK + B_QK
OFF_CU = OFF_BV + B_V
OFF_CV = OFF_CU + C_U

N_EXPERTS = 16
N_EXPERT_GROUPS = 4
EXPERTS_PER_GROUP = 4
TOP_K = 2
D_EXPERT = 1024

DEEPNORM_ALPHA = (2 * DEPTH) ** 0.25
NORM_EPS = 1e-6
LOG2_E = math.log2(math.e)

LANES = 128
VMEM_LIMIT = 60 * 1024 * 1024

TM = 256
N_TILES = TOK // TM
ADA_TN = 512
LN_ROWS = 16
ATT_CK = 512
ATT_UNROLL = 2
ATT_TQ = 256
MOE_BM = 256
MOE_NB = (TOK * TOP_K + N_EXPERTS * (MOE_BM - 1) + MOE_BM - 1) // MOE_BM
MOE_ROWS = MOE_NB * MOE_BM

NT_DIMS = (((1,), (1,)), ((), ()))


def _rms(x, g):
    return x * lax.rsqrt(jnp.mean(x * x, axis=-1, keepdims=True) + NORM_EPS) * g


def _gelu(x):
    return 0.5 * x * (1.0 + lax.erf(x * (2.0 ** -0.5)))


def _standardize(x):
    mu = jnp.mean(x, axis=-1, keepdims=True)
    xc = x - mu
    var = jnp.mean(xc * xc, axis=-1, keepdims=True)
    return xc * lax.rsqrt(var + NORM_EPS)


def _ada_kernel(c_ref, w_ref, b_ref, o_ref):
    for r in range(2):
        cv = c_ref[r]
        act = cv * jax.nn.sigmoid(cv)
        for j in range(ADA_TN // LANES):
            cols = slice(j * LANES, (j + 1) * LANES)
            o_ref[0, r:r + 1, cols] = jnp.sum(w_ref[0, :, cols] * act, axis=0, keepdims=True) + b_ref[0, :, cols]


def _ada_call(c_rep, w_ada, b_ada):
    n_out = w_ada.shape[-1]
    return pl.pallas_call(
        _ada_kernel,
        grid=(DEPTH, n_out // ADA_TN),
        in_specs=[
            pl.BlockSpec((2, D_MODEL, LANES), lambda l, n: (0, 0, 0)),
            pl.BlockSpec((1, D_MODEL, ADA_TN), lambda l, n: (l, 0, n)),
            pl.BlockSpec((1, 1, ADA_TN), lambda l, n: (l, 0, n)),
        ],
        out_specs=pl.BlockSpec((1, 2, ADA_TN), lambda l, n: (l, 0, n)),
        out_shape=jax.ShapeDtypeStruct((DEPTH, 2, n_out), F32),
        compiler_params=pltpu.CompilerParams(dimension_semantics=("arbitrary", "arbitrary"),
                                             vmem_limit_bytes=VMEM_LIMIT),
        name="ada_ln",
    )(c_rep, w_ada, b_ada.reshape(DEPTH, 1, n_out))


def _rope(x, cos, sin_signed, half, lane):
    fwd = pltpu.roll(x, LANES - half, 1)
    bwd = pltpu.roll(x, half, 1)
    partner = jnp.where((lane & (2 * half - 1)) < half, fwd, bwd)
    return x * cos + partner * sin_signed


def _rope_t(x, cos, sin_signed, half):
    blocks = [x[b * half:(b + 1) * half] for b in range(HEAD_DIM // half)]
    partner = jnp.concatenate([blocks[b ^ 1] for b in range(len(blocks))], axis=0)
    return x * cos + partner * sin_signed


T_QA = 0
T_QB = T_QA + A_Q
T_VA = T_QB + B_QK
T_VB = T_VA + A_KV
T_ROWS = T_VB + B_V
R_KA, R_KB, R_CU, R_CV = OFF_AK, OFF_BK, OFF_CU, OFF_CV
T_SEG = 512
WT_BLK = 256
WT_SRC_BLOCKS = tuple(c // WT_BLK for off, n in ((OFF_AQ, A_Q), (OFF_BQ, B_QK), (OFF_AV, A_KV), (OFF_BV, B_V))
                      for c in range(off, off + n, WT_BLK))


def _wt_kernel(blk_ref, w_ref, o_ref):
    del blk_ref
    o_ref[0] = w_ref[0].T.astype(BF16)


def _wt_call(w_in):
    blocks = jnp.asarray(WT_SRC_BLOCKS, jnp.int32)
    return pl.pallas_call(
        _wt_kernel,
        grid_spec=pltpu.PrefetchScalarGridSpec(
            num_scalar_prefetch=1,
            grid=(DEPTH, len(WT_SRC_BLOCKS)),
            in_specs=[pl.BlockSpec((1, D_MODEL, WT_BLK), lambda l, j, blk: (l, 0, blk[j]))],
            out_specs=pl.BlockSpec((1, WT_BLK, D_MODEL), lambda l, j, blk: (l, j, 0)),
        ),
        out_shape=jax.ShapeDtypeStruct((DEPTH, T_ROWS, D_MODEL), BF16),
        compiler_params=pltpu.CompilerParams(dimension_semantics=("arbitrary", "arbitrary")),
        name="w_in_transpose",
    )(blocks, w_in)


def _inproj_kernel(x_ref, mod_ref, wr_ref, wt_ref, cosa_ref, sina_ref, cosb_ref, sinb_ref,
                   cosat_ref, sinat_ref, cosbt_ref, sinbt_ref, gqt_ref, gk_ref, ws_ref, bs_ref, gc_ref,
                   qar_ref, qaf_ref, ka_ref, va_ref, qbr_ref, qbf_ref, kb_ref, vb_ref, yc_ref, pt_sc):
    x = x_ref[...]
    h = (x * (1.0 + mod_ref[0, 1:2, :]) + mod_ref[0, 0:1, :]).astype(BF16)
    lane = lax.broadcasted_iota(jnp.int32, (TM, LANES), 1)
    cosa, sina = cosa_ref[...], sina_ref[...]
    cosb, sinb = cosb_ref[...], sinb_ref[...]
    scale_a = HEAD_DIM ** -0.5 * LOG2_E
    scale_b = B_QK_DIM ** -0.5 * LOG2_E

    for r0 in range(0, T_ROWS, T_SEG):
        n = min(T_SEG, T_ROWS - r0)
        pt_sc[r0:r0 + n, :] = lax.dot_general(wt_ref[0, r0:r0 + n, :], h, NT_DIMS, preferred_element_type=F32)
    for j in range(A_HEADS):
        q = pt_sc[T_QA + j * HEAD_DIM:T_QA + (j + 1) * HEAD_DIM, :]
        qn = q * lax.rsqrt(jnp.mean(q * q, axis=0, keepdims=True) + NORM_EPS) * (gqt_ref[...] * scale_a)
        qaf_ref[j] = qn.astype(BF16)
        qar_ref[j] = _rope_t(qn, cosat_ref[...], sinat_ref[...], 32).astype(BF16)
    zeros_half = jnp.zeros((B_QK_DIM, TM), BF16)
    for j in range(B_HEADS):
        qs = pt_sc[T_QB + j * HEAD_DIM:T_QB + (j + 1) * HEAD_DIM, :] * scale_b
        for q, dst in ((qs, qbf_ref), (_rope_t(qs, cosbt_ref[...], sinbt_ref[...], 16), qbr_ref)):
            qb = q.astype(BF16)
            dst[2 * j] = jnp.concatenate([qb[:B_QK_DIM], zeros_half], axis=0)
            dst[2 * j + 1] = jnp.concatenate([zeros_half, qb[B_QK_DIM:]], axis=0)
    for j in range(A_KV_HEADS):
        va_ref[j] = pt_sc[T_VA + j * HEAD_DIM:T_VA + (j + 1) * HEAD_DIM, :].astype(BF16)
    for j in range(B_HEADS):
        vb_ref[j] = pt_sc[T_VB + j * HEAD_DIM:T_VB + (j + 1) * HEAD_DIM, :].astype(BF16)

    def proj(col):
        p = jnp.dot(h, wr_ref[0, :, col:col + 2 * LANES], preferred_element_type=F32)
        return p[:, :LANES], p[:, LANES:]

    for j, k in enumerate(proj(R_KA)):
        ka_ref[j] = _rope(_rms(k, gk_ref[...]), cosa, sina, 32, lane).astype(BF16)
    for j2 in range(B_HEADS // 2):
        for j, k in zip((2 * j2, 2 * j2 + 1), proj(R_KB + j2 * 2 * LANES)):
            kb_ref[j] = _rope(k, cosb, sinb, 16, lane).astype(BF16)
    for j2 in range(C_GROUPS // 2):
        us = proj(R_CU + j2 * 2 * LANES)
        vs = proj(R_CV + j2 * 2 * LANES)
        for g, u, v in zip((2 * j2, 2 * j2 + 1), us, vs):
            u = _gelu(u)
            v = _standardize(_gelu(v)).astype(BF16)
            for c in range(TM // CHUNK):
                rows = slice(c * CHUNK, (c + 1) * CHUNK)
                mixed = jnp.dot(ws_ref[g], v[rows], preferred_element_type=F32) + bs_ref[g]
                yc_ref[rows, g * LANES:(g + 1) * LANES] = _rms(u[rows] * mixed, gc_ref[...]).astype(BF16)


def _inproj_call(layer, xs, mod, w_rows, w_t, tabs, tabs_t, gq_t, gk, ws, bs, gc):
    def heads(n):
        return (jax.ShapeDtypeStruct((n, TOK, HEAD_DIM), BF16),
                pl.BlockSpec((n, TM, HEAD_DIM), lambda i: (0, i, 0)))

    def heads_t(n):
        return (jax.ShapeDtypeStruct((n, HEAD_DIM, TOK), BF16),
                pl.BlockSpec((n, HEAD_DIM, TM), lambda i: (0, 0, i)))

    outs = [heads_t(A_HEADS), heads_t(A_HEADS), heads(A_KV_HEADS), heads_t(A_KV_HEADS),
            heads_t(2 * B_HEADS), heads_t(2 * B_HEADS), heads(B_HEADS), heads_t(B_HEADS),
            (jax.ShapeDtypeStruct((TOK, C_U), BF16), pl.BlockSpec((TM, C_U), lambda i: (i, 0)))]
    tab_spec = pl.BlockSpec((TM, LANES), lambda i: (i, 0))
    tab_t_spec = pl.BlockSpec((HEAD_DIM, TM), lambda i: (0, i))
    vec_spec = pl.BlockSpec((1, LANES), lambda i: (0, 0))
    return pl.pallas_call(
        _inproj_kernel,
        grid=(N_TILES,),
        in_specs=[
            pl.BlockSpec((TM, D_MODEL), lambda i: (i, 0)),
            pl.BlockSpec((1, 6, D_MODEL), lambda i: (jnp.minimum(i, 1), 0, 0)),
            pl.BlockSpec((1, D_MODEL, D_IN), lambda i: (layer, 0, 0), pipeline_mode=pl.Buffered(1)),
            pl.BlockSpec((1, T_ROWS, D_MODEL), lambda i: (layer, 0, 0), pipeline_mode=pl.Buffered(1)),
            tab_spec, tab_spec, tab_spec, tab_spec,
            tab_t_spec, tab_t_spec, tab_t_spec, tab_t_spec,
            pl.BlockSpec((HEAD_DIM, TM), lambda i: (0, 0)),
            vec_spec,
            pl.BlockSpec((C_GROUPS, CHUNK, CHUNK), lambda i: (0, 0, 0)),
            pl.BlockSpec((C_GROUPS, CHUNK, LANES), lambda i: (0, 0, 0)),
            vec_spec,
        ],
        out_specs=[o[1] for o in outs],
        out_shape=[o[0] for o in outs],
        scratch_shapes=[pltpu.VMEM((T_ROWS, TM), F32)],
        compiler_params=pltpu.CompilerParams(dimension_semantics=("arbitrary",), vmem_limit_bytes=VMEM_LIMIT),
        name="in_proj",
    )(xs, mod, w_rows, w_t, *tabs, *tabs_t, gq_t, gk, ws, bs, gc)


def _attn_kernel(*refs, groups, kv_heads, diff, lam_init):
    if diff:
        qr_ref, qf_ref, k_ref, vt_ref, lam_ref, g_ref, o_ref, m_sc, l_sc, acc_sc, s_sc, p_sc, a_sc, c_sc = refs
    else:
        qr_ref, qf_ref, k_ref, vt_ref, g_ref, o_ref, m_sc, l_sc, acc_sc, s_sc, p_sc, a_sc, c_sc = refs
    i = pl.program_id(1)
    n_chunks = SEQ // ATT_CK
    kv_of = lambda g: g // (groups // kv_heads)

    for g in range(groups):
        kc = k_ref[kv_of(g), 0:CTX, :]
        vc = vt_ref[kv_of(g), :, 0:CTX]
        s = jnp.dot(kc, qf_ref[g], preferred_element_type=F32)
        m0 = jnp.max(s, axis=0, keepdims=True)
        p = jnp.exp2(s - m0)
        m_sc[g] = m0
        l_sc[g] = jnp.sum(p, axis=0, keepdims=True)
        acc_sc[g] = jnp.dot(vc, p.astype(BF16), preferred_element_type=F32)

    def chunk_off(c):
        return pl.multiple_of(CTX + jnp.clip(c, 0, n_chunks - 1) * ATT_CK, CTX)

    def scores(g, c, slot):
        s = jnp.dot(k_ref[kv_of(g), pl.ds(chunk_off(c), ATT_CK), :], qr_ref[g], preferred_element_type=F32)
        s_sc[2 * g + slot] = s
        c_sc[2 * g + slot] = jnp.max(s, axis=0, keepdims=True)

    def softmax(g, slot):
        m_prev = m_sc[g]
        m_new = jnp.maximum(m_prev, c_sc[2 * g + slot])
        alpha = jnp.exp2(m_prev - m_new)
        p = jnp.exp2(s_sc[2 * g + slot] - m_new)
        l_sc[g] = alpha * l_sc[g] + jnp.sum(p, axis=0, keepdims=True)
        m_sc[g] = m_new
        a_sc[2 * g + slot] = alpha
        p_sc[2 * g + slot] = p.astype(BF16)

    def values(g, c, slot):
        acc_sc[g] = a_sc[2 * g + slot] * acc_sc[g] + jnp.dot(vt_ref[kv_of(g), :, pl.ds(chunk_off(c), ATT_CK)],
                                                             p_sc[2 * g + slot], preferred_element_type=F32)

    @pl.when(i >= CTX // ATT_TQ)
    def _latent_keys():
        for g in range(groups):
            scores(g, 0, 0)
            a_sc[2 * g + 1] = jnp.ones((1, ATT_TQ), F32)
            p_sc[2 * g + 1] = jnp.zeros((ATT_CK, ATT_TQ), BF16)

        def body(t, carry):
            for u in range(ATT_UNROLL):
                c0 = 2 * (ATT_UNROLL * t + u)
                for g in range(groups):
                    scores(g, c0 + 1, 1)
                    softmax(g, 0)
                    values(g, c0 - 1, 1)
                for g in range(groups):
                    scores(g, c0 + 2, 0)
                    softmax(g, 1)
                    values(g, c0, 0)
            return carry

        lax.fori_loop(0, n_chunks // (2 * ATT_UNROLL), body, 0)
        for g in range(groups):
            values(g, n_chunks - 1, 1)

    def finish(o_t, gain):
        o_t = o_t * lax.rsqrt(jnp.mean(o_t * o_t, axis=0, keepdims=True) + NORM_EPS) * gain
        return o_t.T.astype(BF16)

    if diff:
        lam = lam_ref[...]
        lam_val = (jnp.exp(jnp.sum(lam[0:1] * lam[1:2], axis=-1, keepdims=True))
                   - jnp.exp(jnp.sum(lam[2:3] * lam[3:4], axis=-1, keepdims=True)) + lam_init)
        for h in range(kv_heads):
            d = acc_sc[2 * h] / l_sc[2 * h] - lam_val * (acc_sc[2 * h + 1] / l_sc[2 * h + 1])
            o_ref[:, h * HEAD_DIM:(h + 1) * HEAD_DIM] = finish(d, g_ref[...] * (1.0 - lam_init))
    else:
        for g in range(groups):
            o_ref[:, g * HEAD_DIM:(g + 1) * HEAD_DIM] = finish(acc_sc[g] / l_sc[g], g_ref[...])


def _attn_call(qr, qf, k, vt, g_out_t, lam, *, groups, kv_heads, diff, lam_init, name):
    n_steps = k.shape[0] // kv_heads
    k_spec = pl.BlockSpec((kv_heads, TOK, HEAD_DIM), lambda h, i: (h, 0, 0))
    vt_spec = pl.BlockSpec((kv_heads, HEAD_DIM, TOK), lambda h, i: (h, 0, 0))
    q_spec = pl.BlockSpec((groups, HEAD_DIM, ATT_TQ), lambda h, i: (h, 0, i))
    in_specs = [q_spec, q_spec, k_spec, vt_spec]
    args = [qr, qf, k, vt]
    if diff:
        in_specs.append(pl.BlockSpec((4, B_QK_DIM), lambda h, i: (0, 0)))
        args.append(lam)
        out_w = kv_heads * HEAD_DIM
    else:
        out_w = groups * HEAD_DIM
    in_specs.append(pl.BlockSpec((HEAD_DIM, ATT_TQ), lambda h, i: (0, 0)))
    args.append(g_out_t)
    return pl.pallas_call(
        functools.partial(_attn_kernel, groups=groups, kv_heads=kv_heads, diff=diff, lam_init=lam_init),
        grid=(n_steps, TOK // ATT_TQ),
        in_specs=in_specs,
        out_specs=pl.BlockSpec((ATT_TQ, out_w), lambda h, i: (i, h)),
        out_shape=jax.ShapeDtypeStruct((TOK, n_steps * out_w), BF16),
        scratch_shapes=[pltpu.VMEM((groups, 1, ATT_TQ), F32), pltpu.VMEM((groups, 1, ATT_TQ), F32),
                        pltpu.VMEM((groups, HEAD_DIM, ATT_TQ), F32),
                        pltpu.VMEM((2 * groups, ATT_CK, ATT_TQ), F32),
                        pltpu.VMEM((2 * groups, ATT_CK, ATT_TQ), BF16),
                        pltpu.VMEM((2 * groups, 1, ATT_TQ), F32),
                        pltpu.VMEM((2 * groups, 1, ATT_TQ), F32)],
        compiler_params=pltpu.CompilerParams(dimension_semantics=("arbitrary", "arbitrary"),
                                             vmem_limit_bytes=VMEM_LIMIT),
        name=name,
    )(*args)


def _pair_max(vals):
    best = None
    for a in range(len(vals)):
        for b in range(a + 1, len(vals)):
            s = vals[a] + vals[b]
            best = s if best is None else jnp.maximum(best, s)
    return best


def _outproj_kernel(ya_ref, yb_ref, yc_ref, w_ref, x_ref, mod_ref, lng_ref, lnb_ref, wr_ref, rb_ref, tri_ref,
                    x1_ref, h2_ref, route_ref, cnt_ref, carry_sc, mix_sc, y_sc, hi_sc, lo_sc):
    i = pl.program_id(0)

    @pl.when(i == 0)
    def _init():
        carry_sc[...] = jnp.zeros_like(carry_sc)

    mix_sc[:, 0:A_Q] = ya_ref[...]
    mix_sc[:, A_Q:A_Q + B_V] = yb_ref[...]
    mix_sc[:, A_Q + B_V:D_MIX] = yc_ref[...]
    y_sc[...] = jnp.dot(mix_sc[...], w_ref[0], preferred_element_type=F32)

    def rows_step(r, carry):
        rows = pl.ds(pl.multiple_of(r * LN_ROWS, LN_ROWS), LN_ROWS)
        z = DEEPNORM_ALPHA * x_ref[rows, :] + mod_ref[0, 2:3, :] * y_sc[rows, :]
        x1 = _standardize(z) * lng_ref[...] + lnb_ref[...]
        x1_ref[rows, :] = x1
        h2 = x1 * (1.0 + mod_ref[0, 4:5, :]) + mod_ref[0, 3:4, :]
        h2_ref[rows, :] = h2
        hi = h2.astype(BF16)
        hi_sc[rows, :] = hi
        lo_sc[rows, :] = (h2 - hi.astype(F32)).astype(BF16)
        return carry

    lax.fori_loop(0, TM // LN_ROWS, rows_step, 0, unroll=2)

    lg = (jnp.dot(hi_sc[...], wr_ref[0], preferred_element_type=F32)
          + jnp.dot(hi_sc[...], wr_ref[1], preferred_element_type=F32)
          + jnp.dot(lo_sc[...], wr_ref[0], preferred_element_type=F32))
    logits = lg.T[0:N_EXPERTS, :]
    scores = jax.nn.sigmoid(logits)
    biased = scores + rb_ref[...]
    b_rows = [biased[e:e + 1, :] for e in range(N_EXPERTS)]
    s_rows = [scores[e:e + 1, :] for e in range(N_EXPERTS)]
    group_score = [_pair_max(b_rows[g * EXPERTS_PER_GROUP:(g + 1) * EXPERTS_PER_GROUP])
                   for g in range(N_EXPERT_GROUPS)]
    best = group_score[0]
    grp = jnp.zeros_like(best)
    for g in range(1, N_EXPERT_GROUPS):
        better = group_score[g] > best
        grp = jnp.where(better, float(g), grp)
        best = jnp.where(better, group_score[g], best)

    def pick(rows_, j):
        out = rows_[j]
        for g in range(1, N_EXPERT_GROUPS):
            out = jnp.where(grp == float(g), rows_[g * EXPERTS_PER_GROUP + j], out)
        return out

    vb = [pick(b_rows, j) for j in range(EXPERTS_PER_GROUP)]
    vs = [pick(s_rows, j) for j in range(EXPERTS_PER_GROUP)]
    chosen = []
    for j in range(EXPERTS_PER_GROUP):
        rank = jnp.zeros_like(best)
        for k in range(EXPERTS_PER_GROUP):
            if k == j:
                continue
            ahead = (vb[k] > vb[j]) | ((vb[k] == vb[j]) if k < j else False)
            rank = rank + jnp.where(ahead, 1.0, 0.0)
        chosen.append(rank < float(TOP_K))
    loc1 = jnp.full_like(best, float(EXPERTS_PER_GROUP))
    loc2 = jnp.full_like(best, -1.0)
    for j in range(EXPERTS_PER_GROUP):
        loc1 = jnp.where(chosen[j], jnp.minimum(loc1, float(j)), loc1)
        loc2 = jnp.where(chosen[j], jnp.maximum(loc2, float(j)), loc2)
    g1 = jnp.zeros_like(best)
    g2 = jnp.zeros_like(best)
    for j in range(EXPERTS_PER_GROUP):
        g1 = jnp.where(loc1 == float(j), vs[j], g1)
        g2 = jnp.where(loc2 == float(j), vs[j], g2)
    gsum = g1 + g2
    e1 = grp * float(EXPERTS_PER_GROUP) + loc1
    e2 = grp * float(EXPERTS_PER_GROUP) + loc2

    eidx = lax.broadcasted_iota(jnp.int32, (N_EXPERTS, TM), 0).astype(F32)
    is1 = eidx == e1
    is2 = eidx == e2
    sel = jnp.where(is1 | is2, 1.0, 0.0)
    before = jnp.dot(sel.astype(BF16), tri_ref[...], preferred_element_type=F32) + carry_sc[...]
    pos1 = jnp.sum(jnp.where(is1, before, 0.0), axis=0, keepdims=True)
    pos2 = jnp.sum(jnp.where(is2, before, 0.0), axis=0, keepdims=True)
    carry = carry_sc[...] + jnp.sum(sel, axis=1, keepdims=True)
    carry_sc[...] = carry
    cnt_ref[...] = carry[:, :LANES]

    route_ref[0:1, :] = e1
    route_ref[1:2, :] = e2
    route_ref[2:3, :] = pos1
    route_ref[3:4, :] = pos2
    route_ref[4:5, :] = g1 / gsum
    route_ref[5:6, :] = g2 / gsum
    route_ref[6:8, :] = jnp.zeros((2, TM), F32)


def _outproj_call(layer, ya, yb, yc, w_out, xs, mod, ln_g, ln_b, wr_t, rb_rep, tri):
    row_spec = lambda w: pl.BlockSpec((TM, w), lambda i: (i, 0))
    vec_spec = pl.BlockSpec((1, D_MODEL), lambda i: (0, 0))
    return pl.pallas_call(
        _outproj_kernel,
        grid=(N_TILES,),
        in_specs=[
            row_spec(A_Q), row_spec(B_V), row_spec(C_U),
            pl.BlockSpec((1, D_MIX, D_MODEL), lambda i: (layer, 0, 0), pipeline_mode=pl.Buffered(1)),
            row_spec(D_MODEL),
            pl.BlockSpec((1, 6, D_MODEL), lambda i: (jnp.minimum(i, 1), 0, 0)),
            vec_spec, vec_spec,
            pl.BlockSpec((2, D_MODEL, LANES), lambda i: (0, 0, 0)),
            pl.BlockSpec((N_EXPERTS, TM), lambda i: (0, 0)),
            pl.BlockSpec((TM, TM), lambda i: (0, 0)),
        ],
        out_specs=[row_spec(D_MODEL), row_spec(D_MODEL),
                   pl.BlockSpec((8, TM), lambda i: (0, i)),
                   pl.BlockSpec((N_EXPERTS, LANES), lambda i: (0, 0))],
        out_shape=[jax.ShapeDtypeStruct((TOK, D_MODEL), F32), jax.ShapeDtypeStruct((TOK, D_MODEL), F32),
                   jax.ShapeDtypeStruct((8, TOK), F32), jax.ShapeDtypeStruct((N_EXPERTS, LANES), F32)],
        scratch_shapes=[pltpu.VMEM((N_EXPERTS, TM), F32), pltpu.VMEM((TM, D_MIX), BF16),
                        pltpu.VMEM((TM, D_MODEL), F32), pltpu.VMEM((TM, D_MODEL), BF16),
                        pltpu.VMEM((TM, D_MODEL), BF16)],
        compiler_params=pltpu.CompilerParams(dimension_semantics=("arbitrary",), vmem_limit_bytes=VMEM_LIMIT),
        name="out_proj",
    )(ya, yb, yc, w_out, xs, mod, ln_g, ln_b, wr_t, rb_rep, tri)


def _row_copy(src_ref, src_row, dst_ref, dst_row, sem):
    return pltpu.make_async_copy(src_ref.at[pl.ds(src_row, 1)], dst_ref.at[pl.ds(dst_row, 1)], sem)


def _slot_source_kernel(dest_ref, src_ref):
    def clear(s, carry):
        src_ref[s] = 0
        return carry

    def scatter(t, carry):
        src_ref[dest_ref[t]] = t
        src_ref[dest_ref[TOK + t]] = t
        return carry

    lax.fori_loop(0, MOE_ROWS, clear, 0, unroll=8)
    lax.fori_loop(0, TOK, scatter, 0, unroll=8)


def _slot_source_call(dest):
    return pl.pallas_call(
        _slot_source_kernel,
        in_specs=[pl.BlockSpec(memory_space=pltpu.SMEM)],
        out_specs=pl.BlockSpec(memory_space=pltpu.SMEM),
        out_shape=jax.ShapeDtypeStruct((MOE_ROWS,), jnp.int32),
        name="moe_slot_source",
    )(dest)


def _moe_kernel(be_ref, na_ref, src_ref, h_ref, wg_ref, wu_ref, wd_ref, o_ref, x_sc, sems):
    del be_ref
    b = pl.program_id(0)
    n_active = na_ref[0]

    def start_gather(block, slot):
        def body(r, carry):
            _row_copy(h_ref, src_ref[block * MOE_BM + r], x_sc.at[slot], r, sems.at[slot]).start()
            return carry
        lax.fori_loop(0, MOE_BM, body, 0, unroll=8)

    def wait_gather(slot):
        def body(r, carry):
            _row_copy(h_ref, 0, x_sc.at[slot], 0, sems.at[slot]).wait()
            return carry
        lax.fori_loop(0, MOE_BM, body, 0, unroll=8)

    @pl.when(b == 0)
    def _first():
        start_gather(0, 0)

    @pl.when(b + 1 < n_active)
    def _prefetch_next():
        start_gather(b + 1, (b + 1) % 2)

    @pl.when(b >= n_active)
    def _unused_block():
        o_ref[...] = jnp.zeros_like(o_ref)

    @pl.when(b < n_active)
    def _block():
        slot = b % 2
        wait_gather(slot)
        xb = x_sc[slot]
        gate = jnp.dot(xb, wg_ref[0, 0], preferred_element_type=F32)
        up = jnp.dot(xb, wu_ref[0, 0], preferred_element_type=F32)
        act = gate * jax.nn.sigmoid(gate) * up
        o_ref[...] = jnp.dot(act, wd_ref[0, 0], preferred_element_type=F32)


def _moe_call(layer, block_e, n_active, src, h2, w_gate, w_up, w_down):
    w_map = lambda b, be, na, sr: (layer, be[b], 0, 0)
    return pl.pallas_call(
        _moe_kernel,
        grid_spec=pltpu.PrefetchScalarGridSpec(
            num_scalar_prefetch=3,
            grid=(MOE_NB,),
            in_specs=[pl.BlockSpec(memory_space=pl.ANY),
                      pl.BlockSpec((1, 1, D_MODEL, D_EXPERT), w_map),
                      pl.BlockSpec((1, 1, D_MODEL, D_EXPERT), w_map),
                      pl.BlockSpec((1, 1, D_EXPERT, D_MODEL), w_map)],
            out_specs=pl.BlockSpec((MOE_BM, D_MODEL), lambda b, be, na, sr: (b, 0)),
            scratch_shapes=[pltpu.VMEM((2, MOE_BM, D_MODEL), F32), pltpu.SemaphoreType.DMA((2,))],
        ),
        out_shape=jax.ShapeDtypeStruct((MOE_ROWS, D_MODEL), F32),
        compiler_params=pltpu.CompilerParams(dimension_semantics=("arbitrary",), vmem_limit_bytes=VMEM_LIMIT),
        name="moe_experts",
    )(block_e, n_active, src, h2, w_gate, w_up, w_down)


def _combine_kernel(dest_ref, ys_ref, x1_ref, g1_ref, g2_ref, mod_ref, lng_ref, lnb_ref, o_ref, rows_sc, sem, *,
                    first_tile):
    base = (pl.program_id(0) + first_tile) * TM

    def start(t, carry):
        tok = base + t
        _row_copy(ys_ref, dest_ref[tok], rows_sc.at[0], t, sem).start()
        _row_copy(ys_ref, dest_ref[TOK + tok], rows_sc.at[1], t, sem).start()
        return carry

    def wait(t, carry):
        _row_copy(ys_ref, 0, rows_sc.at[0], 0, sem).wait()
        _row_copy(ys_ref, 0, rows_sc.at[1], 0, sem).wait()
        return carry

    lax.fori_loop(0, TM, start, 0)
    lax.fori_loop(0, TM, wait, 0)
    y = g1_ref[...] * rows_sc[0] + g2_ref[...] * rows_sc[1]
    z = DEEPNORM_ALPHA * x1_ref[...] + mod_ref[0, 5:6, :] * y
    o_ref[...] = _standardize(z) * lng_ref[...] + lnb_ref[...]


def _combine_call(dest, ys, x1, g1, g2, mod, ln_g, ln_b, *, first_tile):
    row_spec = lambda w: pl.BlockSpec((TM, w), lambda i, d: (i + first_tile, 0))
    vec_spec = pl.BlockSpec((1, D_MODEL), lambda i, d: (0, 0))
    return pl.pallas_call(
        functools.partial(_combine_kernel, first_tile=first_tile),
        grid_spec=pltpu.PrefetchScalarGridSpec(
            num_scalar_prefetch=1,
            grid=(N_TILES - first_tile,),
            in_specs=[pl.BlockSpec(memory_space=pl.ANY), row_spec(D_MODEL), row_spec(1), row_spec(1),
                      pl.BlockSpec((1, 6, D_MODEL), lambda i, d: (jnp.minimum(i + first_tile, 1), 0, 0)),
                      vec_spec, vec_spec],
            out_specs=pl.BlockSpec((TM, D_MODEL), lambda i, d: (i, 0)),
            scratch_shapes=[pltpu.VMEM((2, TM, D_MODEL), F32), pltpu.SemaphoreType.DMA(())],
        ),
        out_shape=jax.ShapeDtypeStruct((TOK - first_tile * TM, D_MODEL), F32),
        compiler_params=pltpu.CompilerParams(dimension_semantics=("arbitrary",), vmem_limit_bytes=VMEM_LIMIT),
        name="moe_combine",
    )(dest, ys, x1, g1, g2, mod, ln_g, ln_b)


def _rope_tables(dim):
    n_freq = dim // 4
    inv = ROPE_THETA ** (-jnp.arange(n_freq, dtype=F32) / n_freq)
    t = jnp.arange(SEQ)
    rows = (t // GRID_W).astype(F32)
    cols = (t % GRID_W).astype(F32)
    lane = jnp.arange(LANES)
    within = lane % dim
    use_col = within >= dim // 2
    freq = within % n_freq
    second_half = (within % (dim // 2)) >= n_freq
    pos = jnp.where(use_col[None, :], cols[:, None], rows[:, None])
    ang = pos * inv[freq][None, :]
    cos = jnp.cos(ang)
    sin = jnp.where(second_half[None, :], jnp.sin(ang), -jnp.sin(ang))
    cos = jnp.concatenate([jnp.ones((CTX, LANES), F32), cos], axis=0)
    sin = jnp.concatenate([jnp.zeros((CTX, LANES), F32), sin], axis=0)
    return cos, sin


def kernel(x, c, ctx, c_ctx, w_ada, b_ada, w_in, w_out, a_q_norm, a_k_norm, a_out_norm, b_lambda, b_out_norm,
           c_spatial, c_spatial_bias, c_out_norm, ln1_g, ln1_b, ln2_g, ln2_b, w_router, router_bias,
           w_gate, w_up, w_down):
    assert x.shape == (1, SEQ, D_MODEL) and ctx.shape == (1, CTX, D_MODEL)
    cos_a, sin_a = _rope_tables(HEAD_DIM)
    cos_b, sin_b = _rope_tables(B_QK_DIM)
    tabs = (cos_a, sin_a, cos_b, sin_b)

    c_rep = jnp.broadcast_to(jnp.stack([c_ctx, c[0]])[:, :, None], (2, D_MODEL, LANES))
    mods = _ada_call(c_rep, w_ada, b_ada).reshape(DEPTH, 2, 6, D_MODEL)

    w_rows = w_in.astype(BF16)
    w_t = _wt_call(w_in)
    tabs_t = tuple(t.T for t in tabs)
    col_rep = lambda a: jnp.broadcast_to(a[:, :, None], (DEPTH, HEAD_DIM, TM))
    gq_t, ga_t, gb_t = col_rep(a_q_norm), col_rep(a_out_norm), col_rep(b_out_norm)
    w_out_b = w_out.astype(BF16)
    ws_b = c_spatial.astype(BF16)
    bs_rep = jnp.broadcast_to(c_spatial_bias[:, :, :, None], (DEPTH, C_GROUPS, CHUNK, LANES))
    wr_pad = jnp.pad(w_router, ((0, 0), (0, LANES - N_EXPERTS)))
    wr_hi = wr_pad.astype(BF16)
    wr_t = jnp.stack([wr_hi, (wr_pad - wr_hi.astype(F32)).astype(BF16)])
    rb_rep = jnp.broadcast_to(router_bias[:, None], (N_EXPERTS, TM))
    tri = jnp.triu(jnp.ones((TM, TM), BF16), k=1)

    xs = jnp.concatenate([ctx[0], x[0]], axis=0)
    for l in range(DEPTH):
        lam_init = 0.8 - 0.6 * math.exp(-0.3 * l)
        mod = mods[l]
        vec = lambda a: a[l].reshape(1, -1)
        qar, qaf, ka, va, qbr, qbf, kb, vb, yc = _inproj_call(
            l, xs, mod, w_rows, w_t, tabs, tabs_t, gq_t[l], vec(a_k_norm), ws_b[l], bs_rep[l], vec(c_out_norm))
        ya = _attn_call(qar, qaf, ka, va, ga_t[l], None, groups=A_GROUP, kv_heads=1, diff=False,
                        lam_init=lam_init, name="attn_gqa")
        yb = _attn_call(qbr, qbf, kb, vb, gb_t[l], b_lambda[l], groups=4, kv_heads=2, diff=True,
                        lam_init=lam_init, name="attn_diff")
        x1, h2, route, counts = _outproj_call(l, ya, yb, yc, w_out_b, xs, mod, vec(ln1_g), vec(ln1_b),
                                              wr_t, rb_rep, tri)

        cnt = counts[:, 0].astype(jnp.int32)
        padded = (cnt + MOE_BM - 1) // MOE_BM * MOE_BM
        ends = jnp.cumsum(padded)
        starts = ends - padded
        e1 = route[0].astype(jnp.int32)
        e2 = route[1].astype(jnp.int32)
        dest = jnp.concatenate([starts[e1] + route[2].astype(jnp.int32), starts[e2] + route[3].astype(jnp.int32)])
        n_active = (ends[-1] // MOE_BM).astype(jnp.int32)
        blk = jnp.minimum(jnp.arange(MOE_NB, dtype=jnp.int32), n_active - 1)
        block_e = jnp.minimum(jnp.sum(ends[None, :] <= (blk * MOE_BM)[:, None], axis=1), N_EXPERTS - 1).astype(jnp.int32)

        src = _slot_source_call(dest)
        ys = _moe_call(l, block_e, n_active.reshape(1), src, h2, w_gate, w_up, w_down)
        xs = _combine_call(dest, ys, x1, route[4].reshape(TOK, 1), route[5].reshape(TOK, 1), mod,
                           vec(ln2_g), vec(ln2_b), first_tile=1 if l == DEPTH - 1 else 0)
    return xs.reshape(1, SEQ, D_MODEL)
```

```python
import functools
import math

import jax
import jax.numpy as jnp
from jax import lax
from jax.experimental import pallas as pl
from jax.experimental.pallas import tpu as pltpu

F32 = jnp.float32
BF16 = jnp.bfloat16

D_MODEL = 2048
SEQ = 8192
CTX = 256
TOK = CTX + SEQ
DEPTH = 4
GRID_W = 64

HEAD_DIM = 128
A_HEADS = 8
A_KV_HEADS = 2
A_GROUP = A_HEADS // A_KV_HEADS
B_HEADS = 4
B_QK_DIM = 64
C_GROUPS = 4
CHUNK = 128
ROPE_THETA = 10000.0

A_Q = A_HEADS * HEAD_DIM
A_KV = A_KV_HEADS * HEAD_DIM
B_QK = B_HEADS * HEAD_DIM
B_V = B_HEADS * HEAD_DIM
C_U = C_GROUPS * HEAD_DIM
D_IN = A_Q + 2 * A_KV + 2 * B_QK + B_V + 2 * C_U
D_MIX = A_Q + B_V + C_U
OFF_AQ = 0
OFF_AK = OFF_AQ + A_Q
OFF_AV = OFF_AK + A_KV
OFF_BQ = OFF_AV + A_KV
OFF_BK = OFF_BQ + B_QK
OFF_BV = OFF_BK + B_QK
OFF_CU = OFF_BV + B_V
OFF_CV = OFF_CU + C_U

N_EXPERTS = 16
N_EXPERT_GROUPS = 4
EXPERTS_PER_GROUP = 4
TOP_K = 2
D_EXPERT = 1024

DEEPNORM_ALPHA = (2 * DEPTH) ** 0.25
NORM_EPS = 1e-6
LOG2_E = math.log2(math.e)

LANES = 128
VMEM_LIMIT = 60 * 1024 * 1024

TM = 256
N_TILES = TOK // TM
ADA_TN = 512
LN_ROWS = 16
ATT_CK = 512
ATT_UNROLL = 2
ATT_TQ = 256
MOE_BM = 256
MOE_NB = (TOK * TOP_K + N_EXPERTS * (MOE_BM - 1) + MOE_BM - 1) // MOE_BM
MOE_ROWS = MOE_NB * MOE_BM

NT_DIMS = (((1,), (1,)), ((), ()))


def _rms(x, g):
    return x * lax.rsqrt(jnp.mean(x * x, axis=-1, keepdims=True) + NORM_EPS) * g


def _gelu(x):
    return 0.5 * x * (1.0 + lax.erf(x * (2.0 ** -0.5)))


def _standardize(x):
    mu = jnp.mean(x, axis=-1, keepdims=True)
    xc = x - mu
    var = jnp.mean(xc * xc, axis=-1, keepdims=True)
    return xc * lax.rsqrt(var + NORM_EPS)


def _ada_kernel(c_ref, w_ref, b_ref, o_ref):
    for r in range(2):
        cv = c_ref[r]
        act = cv * jax.nn.sigmoid(cv)
        for j in range(ADA_TN // LANES):
            cols = slice(j * LANES, (j + 1) * LANES)
            o_ref[0, r:r + 1, cols] = jnp.sum(w_ref[0, :, cols] * act, axis=0, keepdims=True) + b_ref[0, :, cols]


def _ada_call(c_rep, w_ada, b_ada):
    n_out = w_ada.shape[-1]
    return pl.pallas_call(
        _ada_kernel,
        grid=(DEPTH, n_out // ADA_TN),
        in_specs=[
            pl.BlockSpec((2, D_MODEL, LANES), lambda l, n: (0, 0, 0)),
            pl.BlockSpec((1, D_MODEL, ADA_TN), lambda l, n: (l, 0, n)),
            pl.BlockSpec((1, 1, ADA_TN), lambda l, n: (l, 0, n)),
        ],
        out_specs=pl.BlockSpec((1, 2, ADA_TN), lambda l, n: (l, 0, n)),
        out_shape=jax.ShapeDtypeStruct((DEPTH, 2, n_out), F32),
        compiler_params=pltpu.CompilerParams(dimension_semantics=("arbitrary", "arbitrary"),
                                             vmem_limit_bytes=VMEM_LIMIT),
        name="ada_ln",
    )(c_rep, w_ada, b_ada.reshape(DEPTH, 1, n_out))


def _rope(x, cos, sin_signed, half, lane):
    fwd = pltpu.roll(x, LANES - half, 1)
    bwd = pltpu.roll(x, half, 1)
    partner = jnp.where((lane & (2 * half - 1)) < half, fwd, bwd)
    return x * cos + partner * sin_signed


def _rope_t(x, cos, sin_signed, half):
    blocks = [x[b * half:(b + 1) * half] for b in range(HEAD_DIM // half)]
    partner = jnp.concatenate([blocks[b ^ 1] for b in range(len(blocks))], axis=0)
    return x * cos + partner * sin_signed


T_QA = 0
T_QB = T_QA + A_Q
T_VA = T_QB + B_QK
T_VB = T_VA + A_KV
T_ROWS = T_VB + B_V
R_KA, R_KB, R_CU, R_CV = OFF_AK, OFF_BK, OFF_CU, OFF_CV
T_SEG = 512
WT_BLK = 256
WT_SRC_BLOCKS = tuple(c // WT_BLK for off, n in ((OFF_AQ, A_Q), (OFF_BQ, B_QK), (OFF_AV, A_KV), (OFF_BV, B_V))
                      for c in range(off, off + n, WT_BLK))


def _wt_kernel(blk_ref, w_ref, o_ref):
    del blk_ref
    o_ref[0] = w_ref[0].T.astype(BF16)


def _wt_call(w_in):
    blocks = jnp.asarray(WT_SRC_BLOCKS, jnp.int32)
    return pl.pallas_call(
        _wt_kernel,
        grid_spec=pltpu.PrefetchScalarGridSpec(
            num_scalar_prefetch=1,
            grid=(DEPTH, len(WT_SRC_BLOCKS)),
            in_specs=[pl.BlockSpec((1, D_MODEL, WT_BLK), lambda l, j, blk: (l, 0, blk[j]))],
            out_specs=pl.BlockSpec((1, WT_BLK, D_MODEL), lambda l, j, blk: (l, j, 0)),
        ),
        out_shape=jax.ShapeDtypeStruct((DEPTH, T_ROWS, D_MODEL), BF16),
        compiler_params=pltpu.CompilerParams(dimension_semantics=("arbitrary", "arbitrary")),
        name="w_in_transpose",
    )(blocks, w_in)


def _inproj_kernel(x_ref, mod_ref, wr_ref, wt_ref, cosa_ref, sina_ref, cosb_ref, sinb_ref,
                   cosat_ref, sinat_ref, cosbt_ref, sinbt_ref, gqt_ref, gk_ref, ws_ref, bs_ref, gc_ref,
                   qar_ref, qaf_ref, ka_ref, va_ref, qbr_ref, qbf_ref, kb_ref, vb_ref, yc_ref, pt_sc):
    x = x_ref[...]
    h = (x * (1.0 + mod_ref[0, 1:2, :]) + mod_ref[0, 0:1, :]).astype(BF16)
    lane = lax.broadcasted_iota(jnp.int32, (TM, LANES), 1)
    cosa, sina = cosa_ref[...], sina_ref[...]
    cosb, sinb = cosb_ref[...], sinb_ref[...]
    scale_a = HEAD_DIM ** -0.5 * LOG2_E
    scale_b = B_QK_DIM ** -0.5 * LOG2_E

    for r0 in range(0, T_ROWS, T_SEG):
        n = min(T_SEG, T_ROWS - r0)
        pt_sc[r0:r0 + n, :] = lax.dot_general(wt_ref[0, r0:r0 + n, :], h, NT_DIMS, preferred_element_type=F32)
    for j in range(A_HEADS):
        q = pt_sc[T_QA + j * HEAD_DIM:T_QA + (j + 1) * HEAD_DIM, :]
        qn = q * lax.rsqrt(jnp.mean(q * q, axis=0, keepdims=True) + NORM_EPS) * (gqt_ref[...] * scale_a)
        qaf_ref[j] = qn.astype(BF16)
        qar_ref[j] = _rope_t(qn, cosat_ref[...], sinat_ref[...], 32).astype(BF16)
    zeros_half = jnp.zeros((B_QK_DIM, TM), BF16)
    for j in range(B_HEADS):
        qs = pt_sc[T_QB + j * HEAD_DIM:T_QB + (j + 1) * HEAD_DIM, :] * scale_b
        for q, dst in ((qs, qbf_ref), (_rope_t(qs, cosbt_ref[...], sinbt_ref[...], 16), qbr_ref)):
            qb = q.astype(BF16)
            dst[2 * j] = jnp.concatenate([qb[:B_QK_DIM], zeros_half], axis=0)
            dst[2 * j + 1] = jnp.concatenate([zeros_half, qb[B_QK_DIM:]], axis=0)
    for j in range(A_KV_HEADS):
        va_ref[j] = pt_sc[T_VA + j * HEAD_DIM:T_VA + (j + 1) * HEAD_DIM, :].astype(BF16)
    for j in range(B_HEADS):
        vb_ref[j] = pt_sc[T_VB + j * HEAD_DIM:T_VB + (j + 1) * HEAD_DIM, :].astype(BF16)

    def proj(col):
        p = jnp.dot(h, wr_ref[0, :, col:col + 2 * LANES], preferred_element_type=F32)
        return p[:, :LANES], p[:, LANES:]

    for j, k in enumerate(proj(R_KA)):
        ka_ref[j] = _rope(_rms(k, gk_ref[...]), cosa, sina, 32, lane).astype(BF16)
    for j2 in range(B_HEADS // 2):
        for j, k in zip((2 * j2, 2 * j2 + 1), proj(R_KB + j2 * 2 * LANES)):
            kb_ref[j] = _rope(k, cosb, sinb, 16, lane).astype(BF16)
    for j2 in range(C_GROUPS // 2):
        us = proj(R_CU + j2 * 2 * LANES)
        vs = proj(R_CV + j2 * 2 * LANES)
        for g, u, v in zip((2 * j2, 2 * j2 + 1), us, vs):
            u = _gelu(u)
            v = _standardize(_gelu(v)).astype(BF16)
            for c in range(TM // CHUNK):
                rows = slice(c * CHUNK, (c + 1) * CHUNK)
                mixed = jnp.dot(ws_ref[g], v[rows], preferred_element_type=F32) + bs_ref[g]
                yc_ref[rows, g * LANES:(g + 1) * LANES] = _rms(u[rows] * mixed, gc_ref[...]).astype(BF16)


def _inproj_call(layer, xs, mod, w_rows, w_t, tabs, tabs_t, gq_t, gk, ws, bs, gc):
    def heads(n):
        return (jax.ShapeDtypeStruct((n, TOK, HEAD_DIM), BF16),
                pl.BlockSpec((n, TM, HEAD_DIM), lambda i: (0, i, 0)))

    def heads_t(n):
        return (jax.ShapeDtypeStruct((n, HEAD_DIM, TOK), BF16),
                pl.BlockSpec((n, HEAD_DIM, TM), lambda i: (0, 0, i)))

    outs = [heads_t(A_HEADS), heads_t(A_HEADS), heads(A_KV_HEADS), heads_t(A_KV_HEADS),
            heads_t(2 * B_HEADS), heads_t(2 * B_HEADS), heads(B_HEADS), heads_t(B_HEADS),
            (jax.ShapeDtypeStruct((TOK, C_U), BF16), pl.BlockSpec((TM, C_U), lambda i: (i, 0)))]
    tab_spec = pl.BlockSpec((TM, LANES), lambda i: (i, 0))
    tab_t_spec = pl.BlockSpec((HEAD_DIM, TM), lambda i: (0, i))
    vec_spec = pl.BlockSpec((1, LANES), lambda i: (0, 0))
    return pl.pallas_call(
        _inproj_kernel,
        grid=(N_TILES,),
        in_specs=[
            pl.BlockSpec((TM, D_MODEL), lambda i: (i, 0)),
            pl.BlockSpec((1, 6, D_MODEL), lambda i: (jnp.minimum(i, 1), 0, 0)),
            pl.BlockSpec((1, D_MODEL, D_IN), lambda i: (layer, 0, 0), pipeline_mode=pl.Buffered(1)),
            pl.BlockSpec((1, T_ROWS, D_MODEL), lambda i: (layer, 0, 0), pipeline_mode=pl.Buffered(1)),
            tab_spec, tab_spec, tab_spec, tab_spec,
            tab_t_spec, tab_t_spec, tab_t_spec, tab_t_spec,
            pl.BlockSpec((HEAD_DIM, TM), lambda i: (0, 0)),
            vec_spec,
            pl.BlockSpec((C_GROUPS, CHUNK, CHUNK), lambda i: (0, 0, 0)),
            pl.BlockSpec((C_GROUPS, CHUNK, LANES), lambda i: (0, 0, 0)),
            vec_spec,
        ],
        out_specs=[o[1] for o in outs],
        out_shape=[o[0] for o in outs],
        scratch_shapes=[pltpu.VMEM((T_ROWS, TM), F32)],
        compiler_params=pltpu.CompilerParams(dimension_semantics=("arbitrary",), vmem_limit_bytes=VMEM_LIMIT),
        name="in_proj",
    )(xs, mod, w_rows, w_t, *tabs, *tabs_t, gq_t, gk, ws, bs, gc)


def _attn_kernel(*refs, groups, kv_heads, diff, lam_init):
    if diff:
        qr_ref, qf_ref, k_ref, vt_ref, lam_ref, g_ref, o_ref, m_sc, l_sc, acc_sc, s_sc, p_sc, a_sc, c_sc = refs
    else:
        qr_ref, qf_ref, k_ref, vt_ref, g_ref, o_ref, m_sc, l_sc, acc_sc, s_sc, p_sc, a_sc, c_sc = refs
    i = pl.program_id(1)
    n_chunks = SEQ // ATT_CK
    kv_of = lambda g: g // (groups // kv_heads)

    for g in range(groups):
        s = jnp.dot(k_ref[kv_of(g), 0:CTX, :], qf_ref[g], preferred_element_type=F32)
        m0 = jnp.max(s, axis=0, keepdims=True)
        p = jnp.exp2(s - m0)
        m_sc[g] = m0
        l_sc[g] = jnp.sum(p, axis=0, keepdims=True)
        acc_sc[g] = jnp.zeros((HEAD_DIM, ATT_TQ), F32)
        a_sc[2 * g + 1] = jnp.ones((1, ATT_TQ), F32)
        p_sc[2 * g + 1, 0:CTX, :] = p.astype(BF16)
        p_sc[2 * g + 1, CTX:ATT_CK, :] = jnp.zeros((ATT_CK - CTX, ATT_TQ), BF16)

    def chunk_off(c):
        if isinstance(c, int):
            return 0 if c < 0 else CTX + min(c, n_chunks - 1) * ATT_CK
        latent = CTX + jnp.minimum(c, n_chunks - 1) * ATT_CK
        return pl.multiple_of(jnp.where(c < 0, 0, latent), CTX)

    def scores(g, c, slot):
        s = jnp.dot(k_ref[kv_of(g), pl.ds(chunk_off(c), ATT_CK), :], qr_ref[g], preferred_element_type=F32)
        s_sc[2 * g + slot] = s
        c_sc[2 * g + slot] = jnp.max(s, axis=0, keepdims=True)

    def softmax(g, slot):
        m_prev = m_sc[g]
        m_new = jnp.maximum(m_prev, c_sc[2 * g + slot])
        alpha = jnp.exp2(m_prev - m_new)
        p = jnp.exp2(s_sc[2 * g + slot] - m_new)
        l_sc[g] = alpha * l_sc[g] + jnp.sum(p, axis=0, keepdims=True)
        m_sc[g] = m_new
        a_sc[2 * g + slot] = alpha
        p_sc[2 * g + slot] = p.astype(BF16)

    def values(g, c, slot):
        acc_sc[g] = a_sc[2 * g + slot] * acc_sc[g] + jnp.dot(vt_ref[kv_of(g), :, pl.ds(chunk_off(c), ATT_CK)],
                                                             p_sc[2 * g + slot], preferred_element_type=F32)

    @pl.when(i < CTX // ATT_TQ)
    def _context_queries():
        for g in range(groups):
            values(g, -1, 1)

    @pl.when(i >= CTX // ATT_TQ)
    def _latent_keys():
        for g in range(groups):
            scores(g, 0, 0)

        def body(t, carry):
            for u in range(ATT_UNROLL):
                c0 = 2 * (ATT_UNROLL * t + u)
                for g in range(groups):
                    scores(g, c0 + 1, 1)
                    softmax(g, 0)
                    values(g, c0 - 1, 1)
                for g in range(groups):
                    scores(g, c0 + 2, 0)
                    softmax(g, 1)
                    values(g, c0, 0)
            return carry

        lax.fori_loop(0, n_chunks // (2 * ATT_UNROLL), body, 0)
        for g in range(groups):
            values(g, n_chunks - 1, 1)

    def finish(o_t, gain):
        o_t = o_t * lax.rsqrt(jnp.mean(o_t * o_t, axis=0, keepdims=True) + NORM_EPS) * gain
        return o_t.T.astype(BF16)

    if diff:
        lam = lam_ref[...]
        lam_val = (jnp.exp(jnp.sum(lam[0:1] * lam[1:2], axis=-1, keepdims=True))
                   - jnp.exp(jnp.sum(lam[2:3] * lam[3:4], axis=-1, keepdims=True)) + lam_init)
        for h in range(kv_heads):
            d = acc_sc[2 * h] / l_sc[2 * h] - lam_val * (acc_sc[2 * h + 1] / l_sc[2 * h + 1])
            o_ref[:, h * HEAD_DIM:(h + 1) * HEAD_DIM] = finish(d, g_ref[...] * (1.0 - lam_init))
    else:
        for g in range(groups):
            o_ref[:, g * HEAD_DIM:(g + 1) * HEAD_DIM] = finish(acc_sc[g] / l_sc[g], g_ref[...])


def _attn_call(qr, qf, k, vt, g_out_t, lam, *, groups, kv_heads, diff, lam_init, name):
    n_steps = k.shape[0] // kv_heads
    k_spec = pl.BlockSpec((kv_heads, TOK, HEAD_DIM), lambda h, i: (h, 0, 0))
    vt_spec = pl.BlockSpec((kv_heads, HEAD_DIM, TOK), lambda h, i: (h, 0, 0))
    q_spec = pl.BlockSpec((groups, HEAD_DIM, ATT_TQ), lambda h, i: (h, 0, i))
    in_specs = [q_spec, q_spec, k_spec, vt_spec]
    args = [qr, qf, k, vt]
    if diff:
        in_specs.append(pl.BlockSpec((4, B_QK_DIM), lambda h, i: (0, 0)))
        args.append(lam)
        out_w = kv_heads * HEAD_DIM
    else:
        out_w = groups * HEAD_DIM
    in_specs.append(pl.BlockSpec((HEAD_DIM, ATT_TQ), lambda h, i: (0, 0)))
    args.append(g_out_t)
    return pl.pallas_call(
        functools.partial(_attn_kernel, groups=groups, kv_heads=kv_heads, diff=diff, lam_init=lam_init),
        grid=(n_steps, TOK // ATT_TQ),
        in_specs=in_specs,
        out_specs=pl.BlockSpec((ATT_TQ, out_w), lambda h, i: (i, h)),
        out_shape=jax.ShapeDtypeStruct((TOK, n_steps * out_w), BF16),
        scratch_shapes=[pltpu.VMEM((groups, 1, ATT_TQ), F32), pltpu.VMEM((groups, 1, ATT_TQ), F32),
                        pltpu.VMEM((groups, HEAD_DIM, ATT_TQ), F32),
                        pltpu.VMEM((2 * groups, ATT_CK, ATT_TQ), F32),
                        pltpu.VMEM((2 * groups, ATT_CK, ATT_TQ), BF16),
                        pltpu.VMEM((2 * groups, 1, ATT_TQ), F32),
                        pltpu.VMEM((2 * groups, 1, ATT_TQ), F32)],
        compiler_params=pltpu.CompilerParams(dimension_semantics=("arbitrary", "arbitrary"),
                                             vmem_limit_bytes=VMEM_LIMIT),
        name=name,
    )(*args)


def _pair_max(vals):
    best = None
    for a in range(len(vals)):
        for b in range(a + 1, len(vals)):
            s = vals[a] + vals[b]
            best = s if best is None else jnp.maximum(best, s)
    return best


def _outproj_kernel(ya_ref, yb_ref, yc_ref, w_ref, x_ref, mod_ref, lng_ref, lnb_ref, wr_ref, rb_ref, tri_ref,
                    x1_ref, h2_ref, route_ref, cnt_ref, carry_sc, mix_sc, y_sc, hi_sc, lo_sc):
    i = pl.program_id(0)

    @pl.when(i == 0)
    def _init():
        carry_sc[...] = jnp.zeros_like(carry_sc)

    mix_sc[:, 0:A_Q] = ya_ref[...]
    mix_sc[:, A_Q:A_Q + B_V] = yb_ref[...]
    mix_sc[:, A_Q + B_V:D_MIX] = yc_ref[...]
    y_sc[...] = jnp.dot(mix_sc[...], w_ref[0], preferred_element_type=F32)

    def rows_step(r, carry):
        rows = pl.ds(pl.multiple_of(r * LN_ROWS, LN_ROWS), LN_ROWS)
        z = DEEPNORM_ALPHA * x_ref[rows, :] + mod_ref[0, 2:3, :] * y_sc[rows, :]
        x1 = _standardize(z) * lng_ref[...] + lnb_ref[...]
        x1_ref[rows, :] = x1
        h2 = x1 * (1.0 + mod_ref[0, 4:5, :]) + mod_ref[0, 3:4, :]
        h2_ref[rows, :] = h2
        hi = h2.astype(BF16)
        hi_sc[rows, :] = hi
        lo_sc[rows, :] = (h2 - hi.astype(F32)).astype(BF16)
        return carry

    lax.fori_loop(0, TM // LN_ROWS, rows_step, 0, unroll=2)

    lg = (jnp.dot(hi_sc[...], wr_ref[0], preferred_element_type=F32)
          + jnp.dot(hi_sc[...], wr_ref[1], preferred_element_type=F32)
          + jnp.dot(lo_sc[...], wr_ref[0], preferred_element_type=F32))
    logits = lg.T[0:N_EXPERTS, :]
    scores = jax.nn.sigmoid(logits)
    biased = scores + rb_ref[...]
    b_rows = [biased[e:e + 1, :] for e in range(N_EXPERTS)]
    s_rows = [scores[e:e + 1, :] for e in range(N_EXPERTS)]
    group_score = [_pair_max(b_rows[g * EXPERTS_PER_GROUP:(g + 1) * EXPERTS_PER_GROUP])
                   for g in range(N_EXPERT_GROUPS)]
    best = group_score[0]
    grp = jnp.zeros_like(best)
    for g in range(1, N_EXPERT_GROUPS):
        better = group_score[g] > best
        grp = jnp.where(better, float(g), grp)
        best = jnp.where(better, group_score[g], best)

    def pick(rows_, j):
        out = rows_[j]
        for g in range(1, N_EXPERT_GROUPS):
            out = jnp.where(grp == float(g), rows_[g * EXPERTS_PER_GROUP + j], out)
        return out

    vb = [pick(b_rows, j) for j in range(EXPERTS_PER_GROUP)]
    vs = [pick(s_rows, j) for j in range(EXPERTS_PER_GROUP)]
    chosen = []
    for j in range(EXPERTS_PER_GROUP):
        rank = jnp.zeros_like(best)
        for k in range(EXPERTS_PER_GROUP):
            if k == j:
                continue
            ahead = (vb[k] > vb[j]) | ((vb[k] == vb[j]) if k < j else False)
            rank = rank + jnp.where(ahead, 1.0, 0.0)
        chosen.append(rank < float(TOP_K))
    loc1 = jnp.full_like(best, float(EXPERTS_PER_GROUP))
    loc2 = jnp.full_like(best, -1.0)
    for j in range(EXPERTS_PER_GROUP):
        loc1 = jnp.where(chosen[j], jnp.minimum(loc1, float(j)), loc1)
        loc2 = jnp.where(chosen[j], jnp.maximum(loc2, float(j)), loc2)
    g1 = jnp.zeros_like(best)
    g2 = jnp.zeros_like(best)
    for j in range(EXPERTS_PER_GROUP):
        g1 = jnp.where(loc1 == float(j), vs[j], g1)
        g2 = jnp.where(loc2 == float(j), vs[j], g2)
    gsum = g1 + g2
    e1 = grp * float(EXPERTS_PER_GROUP) + loc1
    e2 = grp * float(EXPERTS_PER_GROUP) + loc2

    eidx = lax.broadcasted_iota(jnp.int32, (N_EXPERTS, TM), 0).astype(F32)
    is1 = eidx == e1
    is2 = eidx == e2
    sel = jnp.where(is1 | is2, 1.0, 0.0)
    before = jnp.dot(sel.astype(BF16), tri_ref[...], preferred_element_type=F32) + carry_sc[...]
    pos1 = jnp.sum(jnp.where(is1, before, 0.0), axis=0, keepdims=True)
    pos2 = jnp.sum(jnp.where(is2, before, 0.0), axis=0, keepdims=True)
    carry = carry_sc[...] + jnp.sum(sel, axis=1, keepdims=True)
    carry_sc[...] = carry
    cnt_ref[...] = carry[:, :LANES]

    route_ref[0:1, :] = e1
    route_ref[1:2, :] = e2
    route_ref[2:3, :] = pos1
    route_ref[3:4, :] = pos2
    route_ref[4:5, :] = g1 / gsum
    route_ref[5:6, :] = g2 / gsum
    route_ref[6:8, :] = jnp.zeros((2, TM), F32)


def _outproj_call(layer, ya, yb, yc, w_out, xs, mod, ln_g, ln_b, wr_t, rb_rep, tri):
    row_spec = lambda w: pl.BlockSpec((TM, w), lambda i: (i, 0))
    vec_spec = pl.BlockSpec((1, D_MODEL), lambda i: (0, 0))
    return pl.pallas_call(
        _outproj_kernel,
        grid=(N_TILES,),
        in_specs=[
            row_spec(A_Q), row_spec(B_V), row_spec(C_U),
            pl.BlockSpec((1, D_MIX, D_MODEL), lambda i: (layer, 0, 0), pipeline_mode=pl.Buffered(1)),
            row_spec(D_MODEL),
            pl.BlockSpec((1, 6, D_MODEL), lambda i: (jnp.minimum(i, 1), 0, 0)),
            vec_spec, vec_spec,
            pl.BlockSpec((2, D_MODEL, LANES), lambda i: (0, 0, 0)),
            pl.BlockSpec((N_EXPERTS, TM), lambda i: (0, 0)),
            pl.BlockSpec((TM, TM), lambda i: (0, 0)),
        ],
        out_specs=[row_spec(D_MODEL), row_spec(D_MODEL),
                   pl.BlockSpec((8, TM), lambda i: (0, i)),
                   pl.BlockSpec((N_EXPERTS, LANES), lambda i: (0, 0))],
        out_shape=[jax.ShapeDtypeStruct((TOK, D_MODEL), F32), jax.ShapeDtypeStruct((TOK, D_MODEL), F32),
                   jax.ShapeDtypeStruct((8, TOK), F32), jax.ShapeDtypeStruct((N_EXPERTS, LANES), F32)],
        scratch_shapes=[pltpu.VMEM((N_EXPERTS, TM), F32), pltpu.VMEM((TM, D_MIX), BF16),
                        pltpu.VMEM((TM, D_MODEL), F32), pltpu.VMEM((TM, D_MODEL), BF16),
                        pltpu.VMEM((TM, D_MODEL), BF16)],
        compiler_params=pltpu.CompilerParams(dimension_semantics=("arbitrary",), vmem_limit_bytes=VMEM_LIMIT),
        name="out_proj",
    )(ya, yb, yc, w_out, xs, mod, ln_g, ln_b, wr_t, rb_rep, tri)


def _row_copy(src_ref, src_row, dst_ref, dst_row, sem):
    return pltpu.make_async_copy(src_ref.at[pl.ds(src_row, 1)], dst_ref.at[pl.ds(dst_row, 1)], sem)


def _slot_source_kernel(dest_ref, src_ref):
    def clear(s, carry):
        src_ref[s] = 0
        return carry

    def scatter(t, carry):
        src_ref[dest_ref[t]] = t
        src_ref[dest_ref[TOK + t]] = t
        return carry

    lax.fori_loop(0, MOE_ROWS, clear, 0, unroll=8)
    lax.fori_loop(0, TOK, scatter, 0, unroll=8)


def _slot_source_call(dest):
    return pl.pallas_call(
        _slot_source_kernel,
        in_specs=[pl.BlockSpec(memory_space=pltpu.SMEM)],
        out_specs=pl.BlockSpec(memory_space=pltpu.SMEM),
        out_shape=jax.ShapeDtypeStruct((MOE_ROWS,), jnp.int32),
        name="moe_slot_source",
    )(dest)


def _moe_kernel(be_ref, na_ref, src_ref, h_ref, wg_ref, wu_ref, wd_ref, o_ref, x_sc, sems):
    del be_ref
    b = pl.program_id(0)
    n_active = na_ref[0]

    def start_gather(block, slot):
        def body(r, carry):
            _row_copy(h_ref, src_ref[block * MOE_BM + r], x_sc.at[slot], r, sems.at[slot]).start()
            return carry
        lax.fori_loop(0, MOE_BM, body, 0, unroll=8)

    def wait_gather(slot):
        def body(r, carry):
            _row_copy(h_ref, 0, x_sc.at[slot], 0, sems.at[slot]).wait()
            return carry
        lax.fori_loop(0, MOE_BM, body, 0, unroll=8)

    @pl.when(b == 0)
    def _first():
        start_gather(0, 0)

    @pl.when((b >= n_active) & (b < MOE_NB))
    def _unused_block():
        o_ref[...] = jnp.zeros_like(o_ref)

    @pl.when(b == n_active)
    def _drain():
        wait_gather(b % 2)

    @pl.when(b < n_active)
    def _block():
        slot = b % 2
        wait_gather(slot)
        xb = x_sc[slot]
        gate = jnp.dot(xb, wg_ref[0, 0], preferred_element_type=F32)
        up = jnp.dot(xb, wu_ref[0, 0], preferred_element_type=F32)
        act = gate * jax.nn.sigmoid(gate) * up
        nxt = jnp.minimum(b + 1, n_active - 1) * MOE_BM
        for r in range(MOE_BM):
            _row_copy(h_ref, src_ref[nxt + r], x_sc.at[1 - slot], r, sems.at[1 - slot]).start()
        o_ref[...] = jnp.dot(act, wd_ref[0, 0], preferred_element_type=F32)


def _moe_call(layer, block_e, n_active, src, h2, w_gate, w_up, w_down):
    w_map = lambda b, be, na, sr: (layer, be[b], 0, 0)
    return pl.pallas_call(
        _moe_kernel,
        grid_spec=pltpu.PrefetchScalarGridSpec(
            num_scalar_prefetch=3,
            grid=(MOE_NB + 1,),
            in_specs=[pl.BlockSpec(memory_space=pl.ANY),
                      pl.BlockSpec((1, 1, D_MODEL, D_EXPERT), w_map),
                      pl.BlockSpec((1, 1, D_MODEL, D_EXPERT), w_map),
                      pl.BlockSpec((1, 1, D_EXPERT, D_MODEL), w_map)],
            out_specs=pl.BlockSpec((MOE_BM, D_MODEL), lambda b, be, na, sr: (jnp.minimum(b, MOE_NB - 1), 0)),
            scratch_shapes=[pltpu.VMEM((2, MOE_BM, D_MODEL), F32), pltpu.SemaphoreType.DMA((2,))],
        ),
        out_shape=jax.ShapeDtypeStruct((MOE_ROWS, D_MODEL), F32),
        compiler_params=pltpu.CompilerParams(dimension_semantics=("arbitrary",), vmem_limit_bytes=VMEM_LIMIT),
        name="moe_experts",
    )(block_e, n_active, src, h2, w_gate, w_up, w_down)


def _combine_kernel(dest_ref, ys_ref, x1_ref, g1_ref, g2_ref, mod_ref, lng_ref, lnb_ref, o_ref, rows_sc, sem, *,
                    first_tile):
    base = (pl.program_id(0) + first_tile) * TM

    def start(t, carry):
        tok = base + t
        _row_copy(ys_ref, dest_ref[tok], rows_sc.at[0], t, sem).start()
        _row_copy(ys_ref, dest_ref[TOK + tok], rows_sc.at[1], t, sem).start()
        return carry

    def wait(t, carry):
        _row_copy(ys_ref, 0, rows_sc.at[0], 0, sem).wait()
        _row_copy(ys_ref, 0, rows_sc.at[1], 0, sem).wait()
        return carry

    lax.fori_loop(0, TM, start, 0)
    lax.fori_loop(0, TM, wait, 0)
    y = g1_ref[...] * rows_sc[0] + g2_ref[...] * rows_sc[1]
    z = DEEPNORM_ALPHA * x1_ref[...] + mod_ref[0, 5:6, :] * y
    o_ref[...] = _standardize(z) * lng_ref[...] + lnb_ref[...]


def _combine_call(dest, ys, x1, g1, g2, mod, ln_g, ln_b, *, first_tile):
    row_spec = lambda w: pl.BlockSpec((TM, w), lambda i, d: (i + first_tile, 0))
    vec_spec = pl.BlockSpec((1, D_MODEL), lambda i, d: (0, 0))
    return pl.pallas_call(
        functools.partial(_combine_kernel, first_tile=first_tile),
        grid_spec=pltpu.PrefetchScalarGridSpec(
            num_scalar_prefetch=1,
            grid=(N_TILES - first_tile,),
            in_specs=[pl.BlockSpec(memory_space=pl.ANY), row_spec(D_MODEL), row_spec(1), row_spec(1),
                      pl.BlockSpec((1, 6, D_MODEL), lambda i, d: (jnp.minimum(i + first_tile, 1), 0, 0)),
                      vec_spec, vec_spec],
            out_specs=pl.BlockSpec((TM, D_MODEL), lambda i, d: (i, 0)),
            scratch_shapes=[pltpu.VMEM((2, TM, D_MODEL), F32), pltpu.SemaphoreType.DMA(())],
        ),
        out_shape=jax.ShapeDtypeStruct((TOK - first_tile * TM, D_MODEL), F32),
        compiler_params=pltpu.CompilerParams(dimension_semantics=("arbitrary",), vmem_limit_bytes=VMEM_LIMIT),
        name="moe_combine",
    )(dest, ys, x1, g1, g2, mod, ln_g, ln_b)


def _rope_tables(dim):
    n_freq = dim // 4
    inv = ROPE_THETA ** (-jnp.arange(n_freq, dtype=F32) / n_freq)
    t = jnp.arange(SEQ)
    rows = (t // GRID_W).astype(F32)
    cols = (t % GRID_W).astype(F32)
    lane = jnp.arange(LANES)
    within = lane % dim
    use_col = within >= dim // 2
    freq = within % n_freq
    second_half = (within % (dim // 2)) >= n_freq
    pos = jnp.where(use_col[None, :], cols[:, None], rows[:, None])
    ang = pos * inv[freq][None, :]
    cos = jnp.cos(ang)
    sin = jnp.where(second_half[None, :], jnp.sin(ang), -jnp.sin(ang))
    cos = jnp.concatenate([jnp.ones((CTX, LANES), F32), cos], axis=0)
    sin = jnp.concatenate([jnp.zeros((CTX, LANES), F32), sin], axis=0)
    return cos, sin


def kernel(x, c, ctx, c_ctx, w_ada, b_ada, w_in, w_out, a_q_norm, a_k_norm, a_out_norm, b_lambda, b_out_norm,
           c_spatial, c_spatial_bias, c_out_norm, ln1_g, ln1_b, ln2_g, ln2_b, w_router, router_bias,
           w_gate, w_up, w_down):
    assert x.shape == (1, SEQ, D_MODEL) and ctx.shape == (1, CTX, D_MODEL)
    cos_a, sin_a = _rope_tables(HEAD_DIM)
    cos_b, sin_b = _rope_tables(B_QK_DIM)
    tabs = (cos_a, sin_a, cos_b, sin_b)

    c_rep = jnp.broadcast_to(jnp.stack([c_ctx, c[0]])[:, :, None], (2, D_MODEL, LANES))
    mods = _ada_call(c_rep, w_ada, b_ada).reshape(DEPTH, 2, 6, D_MODEL)

    w_rows = w_in.astype(BF16)
    w_t = _wt_call(w_in)
    tabs_t = tuple(t.T for t in tabs)
    col_rep = lambda a: jnp.broadcast_to(a[:, :, None], (DEPTH, HEAD_DIM, TM))
    gq_t, ga_t, gb_t = col_rep(a_q_norm), col_rep(a_out_norm), col_rep(b_out_norm)
    w_out_b = w_out.astype(BF16)
    ws_b = c_spatial.astype(BF16)
    bs_rep = jnp.broadcast_to(c_spatial_bias[:, :, :, None], (DEPTH, C_GROUPS, CHUNK, LANES))
    wr_pad = jnp.pad(w_router, ((0, 0), (0, LANES - N_EXPERTS)))
    wr_hi = wr_pad.astype(BF16)
    wr_t = jnp.stack([wr_hi, (wr_pad - wr_hi.astype(F32)).astype(BF16)])
    rb_rep = jnp.broadcast_to(router_bias[:, None], (N_EXPERTS, TM))
    tri = jnp.triu(jnp.ones((TM, TM), BF16), k=1)

    xs = jnp.concatenate([ctx[0], x[0]], axis=0)
    for l in range(DEPTH):
        lam_init = 0.8 - 0.6 * math.exp(-0.3 * l)
        mod = mods[l]
        vec = lambda a: a[l].reshape(1, -1)
        qar, qaf, ka, va, qbr, qbf, kb, vb, yc = _inproj_call(
            l, xs, mod, w_rows, w_t, tabs, tabs_t, gq_t[l], vec(a_k_norm), ws_b[l], bs_rep[l], vec(c_out_norm))
        ya = _attn_call(qar, qaf, ka, va, ga_t[l], None, groups=A_GROUP, kv_heads=1, diff=False,
                        lam_init=lam_init, name="attn_gqa")
        yb = _attn_call(qbr, qbf, kb, vb, gb_t[l], b_lambda[l], groups=4, kv_heads=2, diff=True,
                        lam_init=lam_init, name="attn_diff")
        x1, h2, route, counts = _outproj_call(l, ya, yb, yc, w_out_b, xs, mod, vec(ln1_g), vec(ln1_b),
                                              wr_t, rb_rep, tri)

        cnt = counts[:, 0].astype(jnp.int32)
        padded = (cnt + MOE_BM - 1) // MOE_BM * MOE_BM
        ends = jnp.cumsum(padded)
        starts = ends - padded
        e1 = route[0].astype(jnp.int32)
        e2 = route[1].astype(jnp.int32)
        dest = jnp.concatenate([starts[e1] + route[2].astype(jnp.int32), starts[e2] + route[3].astype(jnp.int32)])
        n_active = (ends[-1] // MOE_BM).astype(jnp.int32)
        blk = jnp.minimum(jnp.arange(MOE_NB + 1, dtype=jnp.int32), n_active - 1)
        block_e = jnp.minimum(jnp.sum(ends[None, :] <= (blk * MOE_BM)[:, None], axis=1), N_EXPERTS - 1).astype(jnp.int32)

        src = _slot_source_call(dest)
        ys = _moe_call(l, block_e, n_active.reshape(1), src, h2, w_gate, w_up, w_down)
        xs = _combine_call(dest, ys, x1, route[4].reshape(TOK, 1), route[5].reshape(TOK, 1), mod,
                           vec(ln2_g), vec(ln2_b), first_tile=1 if l == DEPTH - 1 else 0)
    return xs.reshape(1, SEQ, D_MODEL)
```

```python
import functools
import math

import jax
import jax.numpy as jnp
from jax import lax
from jax.experimental import pallas as pl
from jax.experimental.pallas import tpu as pltpu

F32 = jnp.float32
BF16 = jnp.bfloat16

D_MODEL = 2048
SEQ = 8192
CTX = 256
TOK = CTX + SEQ
DEPTH = 4
GRID_W = 64

HEAD_DIM = 128
A_HEADS = 8
A_KV_HEADS = 2
A_GROUP = A_HEADS // A_KV_HEADS
B_HEADS = 4
B_QK_DIM = 64
C_GROUPS = 4
CHUNK = 128
ROPE_THETA = 10000.0

A_Q = A_HEADS * HEAD_DIM
A_KV = A_KV_HEADS * HEAD_DIM
B_QK = B_HEADS * HEAD_DIM
B_V = B_HEADS * HEAD_DIM
C_U = C_GROUPS * HEAD_DIM
D_IN = A_Q + 2 * A_KV + 2 * B_QK + B_V + 2 * C_U
D_MIX = A_Q + B_V + C_U
OFF_AQ = 0
OFF_AK = OFF_AQ + A_Q
OFF_AV = OFF_AK + A_KV
OFF_BQ = OFF_AV + A_KV
OFF_BK = OFF_BQ + B_QK
OFF_BV = OFF_BK + B_QK
OFF_CU = OFF_BV + B_V
OFF_CV = OFF_CU + C_U

N_EXPERTS = 16
N_EXPERT_GROUPS = 4
EXPERTS_PER_GROUP = 4
TOP_K = 2
D_EXPERT = 1024

DEEPNORM_ALPHA = (2 * DEPTH) ** 0.25
NORM_EPS = 1e-6
LOG2_E = math.log2(math.e)

LANES = 128
VMEM_LIMIT = 60 * 1024 * 1024

TM = 256
N_TILES = TOK // TM
ADA_TN = 512
LN_ROWS = 16
ATT_CK = 512
ATT_UNROLL = 2
ATT_TQ = 256
MOE_BM = 256
MOE_NB = (TOK * TOP_K + N_EXPERTS * (MOE_BM - 1) + MOE_BM - 1) // MOE_BM
MOE_ROWS = MOE_NB * MOE_BM

NT_DIMS = (((1,), (1,)), ((), ()))


def _rms(x, g):
    return x * lax.rsqrt(jnp.mean(x * x, axis=-1, keepdims=True) + NORM_EPS) * g


def _gelu(x):
    return 0.5 * x * (1.0 + lax.erf(x * (2.0 ** -0.5)))


def _standardize(x):
    mu = jnp.mean(x, axis=-1, keepdims=True)
    xc = x - mu
    var = jnp.mean(xc * xc, axis=-1, keepdims=True)
    return xc * lax.rsqrt(var + NORM_EPS)


def _ada_kernel(c_ref, w_ref, b_ref, o_ref):
    for r in range(2):
        cv = c_ref[r]
        act = cv * jax.nn.sigmoid(cv)
        for j in range(ADA_TN // LANES):
            cols = slice(j * LANES, (j + 1) * LANES)
            o_ref[0, r:r + 1, cols] = jnp.sum(w_ref[0, :, cols] * act, axis=0, keepdims=True) + b_ref[0, :, cols]


def _ada_call(c_rep, w_ada, b_ada):
    n_out = w_ada.shape[-1]
    return pl.pallas_call(
        _ada_kernel,
        grid=(DEPTH, n_out // ADA_TN),
        in_specs=[
            pl.BlockSpec((2, D_MODEL, LANES), lambda l, n: (0, 0, 0)),
            pl.BlockSpec((1, D_MODEL, ADA_TN), lambda l, n: (l, 0, n)),
            pl.BlockSpec((1, 1, ADA_TN), lambda l, n: (l, 0, n)),
        ],
        out_specs=pl.BlockSpec((1, 2, ADA_TN), lambda l, n: (l, 0, n)),
        out_shape=jax.ShapeDtypeStruct((DEPTH, 2, n_out), F32),
        compiler_params=pltpu.CompilerParams(dimension_semantics=("arbitrary", "arbitrary"),
                                             vmem_limit_bytes=VMEM_LIMIT),
        name="ada_ln",
    )(c_rep, w_ada, b_ada.reshape(DEPTH, 1, n_out))


def _rope(x, cos, sin_signed, half, lane):
    fwd = pltpu.roll(x, LANES - half, 1)
    bwd = pltpu.roll(x, half, 1)
    partner = jnp.where((lane & (2 * half - 1)) < half, fwd, bwd)
    return x * cos + partner * sin_signed


def _rope_t(x, cos, sin_signed, half):
    blocks = [x[b * half:(b + 1) * half] for b in range(HEAD_DIM // half)]
    partner = jnp.concatenate([blocks[b ^ 1] for b in range(len(blocks))], axis=0)
    return x * cos + partner * sin_signed


T_QA = 0
T_QB = T_QA + A_Q
T_VA = T_QB + B_QK
T_VB = T_VA + A_KV
T_ROWS = T_VB + B_V
R_KA, R_KB, R_CU, R_CV = OFF_AK, OFF_BK, OFF_CU, OFF_CV
T_SEG = 512
WT_BLK = 256
WT_SRC_BLOCKS = tuple(c // WT_BLK for off, n in ((OFF_AQ, A_Q), (OFF_BQ, B_QK), (OFF_AV, A_KV), (OFF_BV, B_V))
                      for c in range(off, off + n, WT_BLK))


def _wt_kernel(blk_ref, w_ref, o_ref):
    del blk_ref
    o_ref[0] = w_ref[0].T.astype(BF16)


def _wt_call(w_in):
    blocks = jnp.asarray(WT_SRC_BLOCKS, jnp.int32)
    return pl.pallas_call(
        _wt_kernel,
        grid_spec=pltpu.PrefetchScalarGridSpec(
            num_scalar_prefetch=1,
            grid=(DEPTH, len(WT_SRC_BLOCKS)),
            in_specs=[pl.BlockSpec((1, D_MODEL, WT_BLK), lambda l, j, blk: (l, 0, blk[j]))],
            out_specs=pl.BlockSpec((1, WT_BLK, D_MODEL), lambda l, j, blk: (l, j, 0)),
        ),
        out_shape=jax.ShapeDtypeStruct((DEPTH, T_ROWS, D_MODEL), BF16),
        compiler_params=pltpu.CompilerParams(dimension_semantics=("arbitrary", "arbitrary")),
        name="w_in_transpose",
    )(blocks, w_in)


def _inproj_kernel(x_ref, mod_ref, wr_ref, wt_ref, cosa_ref, sina_ref, cosb_ref, sinb_ref,
                   cosat_ref, sinat_ref, cosbt_ref, sinbt_ref, gqt_ref, gk_ref, ws_ref, bs_ref, gc_ref,
                   qar_ref, qaf_ref, ka_ref, va_ref, qbr_ref, qbf_ref, kb_ref, vb_ref, yc_ref, pt_sc):
    x = x_ref[...]
    h = (x * (1.0 + mod_ref[0, 1:2, :]) + mod_ref[0, 0:1, :]).astype(BF16)
    lane = lax.broadcasted_iota(jnp.int32, (TM, LANES), 1)
    cosa, sina = cosa_ref[...], sina_ref[...]
    cosb, sinb = cosb_ref[...], sinb_ref[...]
    scale_a = HEAD_DIM ** -0.5 * LOG2_E
    scale_b = B_QK_DIM ** -0.5 * LOG2_E

    for r0 in range(0, T_ROWS, T_SEG):
        n = min(T_SEG, T_ROWS - r0)
        pt_sc[r0:r0 + n, :] = lax.dot_general(wt_ref[0, r0:r0 + n, :], h, NT_DIMS, preferred_element_type=F32)
    for j in range(A_HEADS):
        q = pt_sc[T_QA + j * HEAD_DIM:T_QA + (j + 1) * HEAD_DIM, :]
        qn = q * lax.rsqrt(jnp.mean(q * q, axis=0, keepdims=True) + NORM_EPS) * (gqt_ref[...] * scale_a)
        qaf_ref[j] = qn.astype(BF16)
        qar_ref[j] = _rope_t(qn, cosat_ref[...], sinat_ref[...], 32).astype(BF16)
    zeros_half = jnp.zeros((B_QK_DIM, TM), BF16)
    for j in range(B_HEADS):
        qs = pt_sc[T_QB + j * HEAD_DIM:T_QB + (j + 1) * HEAD_DIM, :] * scale_b
        for q, dst in ((qs, qbf_ref), (_rope_t(qs, cosbt_ref[...], sinbt_ref[...], 16), qbr_ref)):
            qb = q.astype(BF16)
            dst[2 * j] = jnp.concatenate([qb[:B_QK_DIM], zeros_half], axis=0)
            dst[2 * j + 1] = jnp.concatenate([zeros_half, qb[B_QK_DIM:]], axis=0)
    for j in range(A_KV_HEADS):
        va_ref[j] = pt_sc[T_VA + j * HEAD_DIM:T_VA + (j + 1) * HEAD_DIM, :].astype(BF16)
    for j in range(B_HEADS):
        vb_ref[j] = pt_sc[T_VB + j * HEAD_DIM:T_VB + (j + 1) * HEAD_DIM, :].astype(BF16)

    def proj(col):
        p = jnp.dot(h, wr_ref[0, :, col:col + 2 * LANES], preferred_element_type=F32)
        return p[:, :LANES], p[:, LANES:]

    for j, k in enumerate(proj(R_KA)):
        ka_ref[j] = _rope(_rms(k, gk_ref[...]), cosa, sina, 32, lane).astype(BF16)
    for j2 in range(B_HEADS // 2):
        for j, k in zip((2 * j2, 2 * j2 + 1), proj(R_KB + j2 * 2 * LANES)):
            kb_ref[j] = _rope(k, cosb, sinb, 16, lane).astype(BF16)
    for j2 in range(C_GROUPS // 2):
        us = proj(R_CU + j2 * 2 * LANES)
        vs = proj(R_CV + j2 * 2 * LANES)
        for g, u, v in zip((2 * j2, 2 * j2 + 1), us, vs):
            u = _gelu(u)
            v = _standardize(_gelu(v)).astype(BF16)
            for c in range(TM // CHUNK):
                rows = slice(c * CHUNK, (c + 1) * CHUNK)
                mixed = jnp.dot(ws_ref[g], v[rows], preferred_element_type=F32) + bs_ref[g]
                yc_ref[rows, g * LANES:(g + 1) * LANES] = _rms(u[rows] * mixed, gc_ref[...]).astype(BF16)


def _inproj_call(layer, xs, mod, w_rows, w_t, tabs, tabs_t, gq_t, gk, ws, bs, gc):
    def heads(n):
        return (jax.ShapeDtypeStruct((n, TOK, HEAD_DIM), BF16),
                pl.BlockSpec((n, TM, HEAD_DIM), lambda i: (0, i, 0)))

    def heads_t(n):
        return (jax.ShapeDtypeStruct((n, HEAD_DIM, TOK), BF16),
                pl.BlockSpec((n, HEAD_DIM, TM), lambda i: (0, 0, i)))

    outs = [heads_t(A_HEADS), heads_t(A_HEADS), heads(A_KV_HEADS), heads_t(A_KV_HEADS),
            heads_t(2 * B_HEADS), heads_t(2 * B_HEADS), heads(B_HEADS), heads_t(B_HEADS),
            (jax.ShapeDtypeStruct((TOK, C_U), BF16), pl.BlockSpec((TM, C_U), lambda i: (i, 0)))]
    tab_spec = pl.BlockSpec((TM, LANES), lambda i: (i, 0))
    tab_t_spec = pl.BlockSpec((HEAD_DIM, TM), lambda i: (0, i))
    vec_spec = pl.BlockSpec((1, LANES), lambda i: (0, 0))
    return pl.pallas_call(
        _inproj_kernel,
        grid=(N_TILES,),
        in_specs=[
            pl.BlockSpec((TM, D_MODEL), lambda i: (i, 0)),
            pl.BlockSpec((1, 6, D_MODEL), lambda i: (jnp.minimum(i, 1), 0, 0)),
            pl.BlockSpec((1, D_MODEL, D_IN), lambda i: (layer, 0, 0), pipeline_mode=pl.Buffered(1)),
            pl.BlockSpec((1, T_ROWS, D_MODEL), lambda i: (layer, 0, 0), pipeline_mode=pl.Buffered(1)),
            tab_spec, tab_spec, tab_spec, tab_spec,
            tab_t_spec, tab_t_spec, tab_t_spec, tab_t_spec,
            pl.BlockSpec((HEAD_DIM, TM), lambda i: (0, 0)),
            vec_spec,
            pl.BlockSpec((C_GROUPS, CHUNK, CHUNK), lambda i: (0, 0, 0)),
            pl.BlockSpec((C_GROUPS, CHUNK, LANES), lambda i: (0, 0, 0)),
            vec_spec,
        ],
        out_specs=[o[1] for o in outs],
        out_shape=[o[0] for o in outs],
        scratch_shapes=[pltpu.VMEM((T_ROWS, TM), F32)],
        compiler_params=pltpu.CompilerParams(dimension_semantics=("arbitrary",), vmem_limit_bytes=VMEM_LIMIT),
        name="in_proj",
    )(xs, mod, w_rows, w_t, *tabs, *tabs_t, gq_t, gk, ws, bs, gc)


def _attn_kernel(*refs, groups, kv_heads, diff, lam_init):
    if diff:
        qr_ref, qf_ref, k_ref, vt_ref, lam_ref, g_ref, o_ref, m_sc, l_sc, acc_sc, s_sc, p_sc, a_sc, c_sc = refs
    else:
        qr_ref, qf_ref, k_ref, vt_ref, g_ref, o_ref, m_sc, l_sc, acc_sc, s_sc, p_sc, a_sc, c_sc = refs
    i = pl.program_id(1)
    n_chunks = SEQ // ATT_CK
    kv_of = lambda g: g // (groups // kv_heads)

    for g in range(groups):
        s = jnp.dot(k_ref[kv_of(g), 0:CTX, :], qf_ref[g], preferred_element_type=F32)
        m0 = jnp.max(s, axis=0, keepdims=True)
        p = jnp.exp2(s - m0)
        m_sc[g] = m0
        l_sc[g] = jnp.sum(p, axis=0, keepdims=True)
        acc_sc[g] = jnp.zeros((HEAD_DIM, ATT_TQ), F32)
        a_sc[2 * g + 1] = jnp.ones((1, ATT_TQ), F32)
        p_sc[2 * g + 1, 0:CTX, :] = p.astype(BF16)
        p_sc[2 * g + 1, CTX:ATT_CK, :] = jnp.zeros((ATT_CK - CTX, ATT_TQ), BF16)

    def chunk_off(c):
        if isinstance(c, int):
            return 0 if c < 0 else CTX + min(c, n_chunks - 1) * ATT_CK
        latent = CTX + jnp.minimum(c, n_chunks - 1) * ATT_CK
        return pl.multiple_of(jnp.where(c < 0, 0, latent), CTX)

    def scores(g, c, slot):
        s = jnp.dot(k_ref[kv_of(g), pl.ds(chunk_off(c), ATT_CK), :], qr_ref[g], preferred_element_type=F32)
        s_sc[2 * g + slot] = s
        c_sc[2 * g + slot] = jnp.max(s, axis=0, keepdims=True)

    def softmax(g, slot):
        m_prev = m_sc[g]
        m_new = jnp.maximum(m_prev, c_sc[2 * g + slot])
        alpha = jnp.exp2(m_prev - m_new)
        p = jnp.exp2(s_sc[2 * g + slot] - m_new)
        l_sc[g] = alpha * l_sc[g] + jnp.sum(p, axis=0, keepdims=True)
        m_sc[g] = m_new
        a_sc[2 * g + slot] = alpha
        p_sc[2 * g + slot] = p.astype(BF16)

    def values(g, c, slot):
        acc_sc[g] = a_sc[2 * g + slot] * acc_sc[g] + jnp.dot(vt_ref[kv_of(g), :, pl.ds(chunk_off(c), ATT_CK)],
                                                             p_sc[2 * g + slot], preferred_element_type=F32)

    @pl.when(i < CTX // ATT_TQ)
    def _context_queries():
        for g in range(groups):
            values(g, -1, 1)

    @pl.when(i >= CTX // ATT_TQ)
    def _latent_keys():
        for g in range(groups):
            scores(g, 0, 0)

        def body(t, carry):
            for u in range(ATT_UNROLL):
                c0 = 2 * (ATT_UNROLL * t + u)
                for g in range(groups):
                    scores(g, c0 + 1, 1)
                    softmax(g, 0)
                    values(g, c0 - 1, 1)
                for g in range(groups):
                    scores(g, c0 + 2, 0)
                    softmax(g, 1)
                    values(g, c0, 0)
            return carry

        lax.fori_loop(0, n_chunks // (2 * ATT_UNROLL), body, 0)
        for g in range(groups):
            values(g, n_chunks - 1, 1)

    def finish(o_t, gain):
        o_t = o_t * lax.rsqrt(jnp.mean(o_t * o_t, axis=0, keepdims=True) + NORM_EPS) * gain
        return o_t.T.astype(BF16)

    if diff:
        lam = lam_ref[...]
        lam_val = (jnp.exp(jnp.sum(lam[0:1] * lam[1:2], axis=-1, keepdims=True))
                   - jnp.exp(jnp.sum(lam[2:3] * lam[3:4], axis=-1, keepdims=True)) + lam_init)
        for h in range(kv_heads):
            d = acc_sc[2 * h] / l_sc[2 * h] - lam_val * (acc_sc[2 * h + 1] / l_sc[2 * h + 1])
            o_ref[:, h * HEAD_DIM:(h + 1) * HEAD_DIM] = finish(d, g_ref[...] * (1.0 - lam_init))
    else:
        for g in range(groups):
            o_ref[:, g * HEAD_DIM:(g + 1) * HEAD_DIM] = finish(acc_sc[g] / l_sc[g], g_ref[...])


def _attn_call(qr, qf, k, vt, g_out_t, lam, *, groups, kv_heads, diff, lam_init, name):
    n_steps = k.shape[0] // kv_heads
    k_spec = pl.BlockSpec((kv_heads, TOK, HEAD_DIM), lambda h, i: (h, 0, 0))
    vt_spec = pl.BlockSpec((kv_heads, HEAD_DIM, TOK), lambda h, i: (h, 0, 0))
    q_spec = pl.BlockSpec((groups, HEAD_DIM, ATT_TQ), lambda h, i: (h, 0, i))
    in_specs = [q_spec, q_spec, k_spec, vt_spec]
    args = [qr, qf, k, vt]
    if diff:
        in_specs.append(pl.BlockSpec((4, B_QK_DIM), lambda h, i: (0, 0)))
        args.append(lam)
        out_w = kv_heads * HEAD_DIM
    else:
        out_w = groups * HEAD_DIM
    in_specs.append(pl.BlockSpec((HEAD_DIM, ATT_TQ), lambda h, i: (0, 0)))
    args.append(g_out_t)
    return pl.pallas_call(
        functools.partial(_attn_kernel, groups=groups, kv_heads=kv_heads, diff=diff, lam_init=lam_init),
        grid=(n_steps, TOK // ATT_TQ),
        in_specs=in_specs,
        out_specs=pl.BlockSpec((ATT_TQ, out_w), lambda h, i: (i, h)),
        out_shape=jax.ShapeDtypeStruct((TOK, n_steps * out_w), BF16),
        scratch_shapes=[pltpu.VMEM((groups, 1, ATT_TQ), F32), pltpu.VMEM((groups, 1, ATT_TQ), F32),
                        pltpu.VMEM((groups, HEAD_DIM, ATT_TQ), F32),
                        pltpu.VMEM((2 * groups, ATT_CK, ATT_TQ), F32),
                        pltpu.VMEM((2 * groups, ATT_CK, ATT_TQ), BF16),
                        pltpu.VMEM((2 * groups, 1, ATT_TQ), F32),
                        pltpu.VMEM((2 * groups, 1, ATT_TQ), F32)],
        compiler_params=pltpu.CompilerParams(dimension_semantics=("arbitrary", "arbitrary"),
                                             vmem_limit_bytes=VMEM_LIMIT),
        name=name,
    )(*args)


def _pair_max(vals):
    best = None
    for a in range(len(vals)):
        for b in range(a + 1, len(vals)):
            s = vals[a] + vals[b]
            best = s if best is None else jnp.maximum(best, s)
    return best


def _outproj_kernel(ya_ref, yb_ref, yc_ref, w_ref, x_ref, mod_ref, lng_ref, lnb_ref, wr_ref, rb_ref, tri_ref,
                    x1_ref, h2_ref, route_ref, cnt_ref, carry_sc, mix_sc, y_sc, hi_sc, lo_sc):
    i = pl.program_id(0)

    @pl.when(i == 0)
    def _init():
        carry_sc[...] = jnp.zeros_like(carry_sc)

    mix_sc[:, 0:A_Q] = ya_ref[...]
    mix_sc[:, A_Q:A_Q + B_V] = yb_ref[...]
    mix_sc[:, A_Q + B_V:D_MIX] = yc_ref[...]
    y_sc[...] = jnp.dot(mix_sc[...], w_ref[0], preferred_element_type=F32)

    def rows_step(r, carry):
        rows = pl.ds(pl.multiple_of(r * LN_ROWS, LN_ROWS), LN_ROWS)
        z = DEEPNORM_ALPHA * x_ref[rows, :] + mod_ref[0, 2:3, :] * y_sc[rows, :]
        x1 = _standardize(z) * lng_ref[...] + lnb_ref[...]
        x1_ref[rows, :] = x1
        h2 = x1 * (1.0 + mod_ref[0, 4:5, :]) + mod_ref[0, 3:4, :]
        h2_ref[rows, :] = h2
        hi = h2.astype(BF16)
        hi_sc[rows, :] = hi
        lo_sc[rows, :] = (h2 - hi.astype(F32)).astype(BF16)
        return carry

    lax.fori_loop(0, TM // LN_ROWS, rows_step, 0, unroll=2)

    lg = (jnp.dot(hi_sc[...], wr_ref[0], preferred_element_type=F32)
          + jnp.dot(hi_sc[...], wr_ref[1], preferred_element_type=F32)
          + jnp.dot(lo_sc[...], wr_ref[0], preferred_element_type=F32))
    logits = lg.T[0:N_EXPERTS, :]
    scores = jax.nn.sigmoid(logits)
    biased = scores + rb_ref[...]
    b_rows = [biased[e:e + 1, :] for e in range(N_EXPERTS)]
    s_rows = [scores[e:e + 1, :] for e in range(N_EXPERTS)]
    group_score = [_pair_max(b_rows[g * EXPERTS_PER_GROUP:(g + 1) * EXPERTS_PER_GROUP])
                   for g in range(N_EXPERT_GROUPS)]
    best = group_score[0]
    grp = jnp.zeros_like(best)
    for g in range(1, N_EXPERT_GROUPS):
        better = group_score[g] > best
        grp = jnp.where(better, float(g), grp)
        best = jnp.where(better, group_score[g], best)

    def pick(rows_, j):
        out = rows_[j]
        for g in range(1, N_EXPERT_GROUPS):
            out = jnp.where(grp == float(g), rows_[g * EXPERTS_PER_GROUP + j], out)
        return out

    vb = [pick(b_rows, j) for j in range(EXPERTS_PER_GROUP)]
    vs = [pick(s_rows, j) for j in range(EXPERTS_PER_GROUP)]
    chosen = []
    for j in range(EXPERTS_PER_GROUP):
        rank = jnp.zeros_like(best)
        for k in range(EXPERTS_PER_GROUP):
            if k == j:
                continue
            ahead = (vb[k] > vb[j]) | ((vb[k] == vb[j]) if k < j else False)
            rank = rank + jnp.where(ahead, 1.0, 0.0)
        chosen.append(rank < float(TOP_K))
    loc1 = jnp.full_like(best, float(EXPERTS_PER_GROUP))
    loc2 = jnp.full_like(best, -1.0)
    for j in range(EXPERTS_PER_GROUP):
        loc1 = jnp.where(chosen[j], jnp.minimum(loc1, float(j)), loc1)
        loc2 = jnp.where(chosen[j], jnp.maximum(loc2, float(j)), loc2)
    g1 = jnp.zeros_like(best)
    g2 = jnp.zeros_like(best)
    for j in range(EXPERTS_PER_GROUP):
        g1 = jnp.where(loc1 == float(j), vs[j], g1)
        g2 = jnp.where(loc2 == float(j), vs[j], g2)
    gsum = g1 + g2
    e1 = grp * float(EXPERTS_PER_GROUP) + loc1
    e2 = grp * float(EXPERTS_PER_GROUP) + loc2

    eidx = lax.broadcasted_iota(jnp.int32, (N_EXPERTS, TM), 0).astype(F32)
    is1 = eidx == e1
    is2 = eidx == e2
    sel = jnp.where(is1 | is2, 1.0, 0.0)
    before = jnp.dot(sel.astype(BF16), tri_ref[...], preferred_element_type=F32) + carry_sc[...]
    pos1 = jnp.sum(jnp.where(is1, before, 0.0), axis=0, keepdims=True)
    pos2 = jnp.sum(jnp.where(is2, before, 0.0), axis=0, keepdims=True)
    carry = carry_sc[...] + jnp.sum(sel, axis=1, keepdims=True)
    carry_sc[...] = carry
    cnt_ref[...] = carry[:, :LANES]

    route_ref[0:1, :] = e1
    route_ref[1:2, :] = e2
    route_ref[2:3, :] = pos1
    route_ref[3:4, :] = pos2
    route_ref[4:5, :] = g1 / gsum
    route_ref[5:6, :] = g2 / gsum
    route_ref[6:8, :] = jnp.zeros((2, TM), F32)


def _outproj_call(layer, ya, yb, yc, w_out, xs, mod, ln_g, ln_b, wr_t, rb_rep, tri):
    row_spec = lambda w: pl.BlockSpec((TM, w), lambda i: (i, 0))
    vec_spec = pl.BlockSpec((1, D_MODEL), lambda i: (0, 0))
    return pl.pallas_call(
        _outproj_kernel,
        grid=(N_TILES,),
        in_specs=[
            row_spec(A_Q), row_spec(B_V), row_spec(C_U),
            pl.BlockSpec((1, D_MIX, D_MODEL), lambda i: (layer, 0, 0), pipeline_mode=pl.Buffered(1)),
            row_spec(D_MODEL),
            pl.BlockSpec((1, 6, D_MODEL), lambda i: (jnp.minimum(i, 1), 0, 0)),
            vec_spec, vec_spec,
            pl.BlockSpec((2, D_MODEL, LANES), lambda i: (0, 0, 0)),
            pl.BlockSpec((N_EXPERTS, TM), lambda i: (0, 0)),
            pl.BlockSpec((TM, TM), lambda i: (0, 0)),
        ],
        out_specs=[row_spec(D_MODEL), row_spec(D_MODEL),
                   pl.BlockSpec((8, TM), lambda i: (0, i)),
                   pl.BlockSpec((N_EXPERTS, LANES), lambda i: (0, 0))],
        out_shape=[jax.ShapeDtypeStruct((TOK, D_MODEL), F32), jax.ShapeDtypeStruct((TOK, D_MODEL), F32),
                   jax.ShapeDtypeStruct((8, TOK), F32), jax.ShapeDtypeStruct((N_EXPERTS, LANES), F32)],
        scratch_shapes=[pltpu.VMEM((N_EXPERTS, TM), F32), pltpu.VMEM((TM, D_MIX), BF16),
                        pltpu.VMEM((TM, D_MODEL), F32), pltpu.VMEM((TM, D_MODEL), BF16),
                        pltpu.VMEM((TM, D_MODEL), BF16)],
        compiler_params=pltpu.CompilerParams(dimension_semantics=("arbitrary",), vmem_limit_bytes=VMEM_LIMIT),
        name="out_proj",
    )(ya, yb, yc, w_out, xs, mod, ln_g, ln_b, wr_t, rb_rep, tri)


def _row_copy(src_ref, src_row, dst_ref, dst_row, sem):
    return pltpu.make_async_copy(src_ref.at[pl.ds(src_row, 1)], dst_ref.at[pl.ds(dst_row, 1)], sem)


def _slot_source_kernel(dest_ref, src_ref):
    def clear(s, carry):
        src_ref[s] = 0
        return carry

    def scatter(t, carry):
        src_ref[dest_ref[t]] = t
        src_ref[dest_ref[TOK + t]] = t
        return carry

    lax.fori_loop(0, MOE_ROWS, clear, 0, unroll=8)
    lax.fori_loop(0, TOK, scatter, 0, unroll=8)


def _slot_source_call(dest):
    return pl.pallas_call(
        _slot_source_kernel,
        in_specs=[pl.BlockSpec(memory_space=pltpu.SMEM)],
        out_specs=pl.BlockSpec(memory_space=pltpu.SMEM),
        out_shape=jax.ShapeDtypeStruct((MOE_ROWS,), jnp.int32),
        name="moe_slot_source",
    )(dest)


def _moe_kernel(be_ref, plan_ref, na_ref, src_ref, h_ref, wg_ref, wu_ref, wd_ref, o_ref,
                x_sc, wg_sc, wu_sc, wd_sc, sems, wsems, *, layer):
    b = pl.program_id(0)
    n_active = na_ref[0]

    def start_gather(block, slot):
        def body(r, carry):
            _row_copy(h_ref, src_ref[block * MOE_BM + r], x_sc.at[slot], r, sems.at[slot]).start()
            return carry
        lax.fori_loop(0, MOE_BM, body, 0, unroll=8)

    def wait_gather(slot):
        def body(r, carry):
            _row_copy(h_ref, 0, x_sc.at[slot], 0, sems.at[slot]).wait()
            return carry
        lax.fori_loop(0, MOE_BM, body, 0, unroll=8)

    def weight_copies(expert, slot):
        return (pltpu.make_async_copy(wg_ref.at[layer, expert], wg_sc.at[slot], wsems.at[slot]),
                pltpu.make_async_copy(wu_ref.at[layer, expert], wu_sc.at[slot], wsems.at[slot]),
                pltpu.make_async_copy(wd_ref.at[layer, expert], wd_sc.at[slot], wsems.at[slot]))

    @pl.when(b == 0)
    def _first():
        for cp in weight_copies(be_ref[0], 0):
            cp.start()
        start_gather(0, 0)

    @pl.when(b + 1 < n_active)
    def _prefetch_next():
        start_gather(b + 1, (b + 1) % 2)

    @pl.when(b >= n_active)
    def _unused_block():
        o_ref[...] = jnp.zeros_like(o_ref)

    new_expert = (b < n_active) & (plan_ref[3 * b] == 1)
    wslot = plan_ref[3 * b + 1]
    next_expert = plan_ref[3 * b + 2]

    @pl.when(new_expert & (next_expert >= 0))
    def _fetch_next_expert():
        for cp in weight_copies(next_expert, 1 - wslot):
            cp.start()

    @pl.when(new_expert)
    def _await_expert():
        for cp in weight_copies(be_ref[b], wslot):
            cp.wait()

    @pl.when(b < n_active)
    def _block():
        slot = b % 2
        wait_gather(slot)
        xb = x_sc[slot]
        gate = jnp.dot(xb, wg_sc[wslot], preferred_element_type=F32)
        up = jnp.dot(xb, wu_sc[wslot], preferred_element_type=F32)
        act = gate * jax.nn.sigmoid(gate) * up
        o_ref[...] = jnp.dot(act, wd_sc[wslot], preferred_element_type=F32)


def _moe_call(layer, block_e, plan, n_active, src, h2, w_gate, w_up, w_down):
    return pl.pallas_call(
        functools.partial(_moe_kernel, layer=layer),
        grid_spec=pltpu.PrefetchScalarGridSpec(
            num_scalar_prefetch=4,
            grid=(MOE_NB,),
            in_specs=[pl.BlockSpec(memory_space=pl.ANY)] * 4,
            out_specs=pl.BlockSpec((MOE_BM, D_MODEL), lambda b, be, pn, na, sr: (b, 0)),
            scratch_shapes=[pltpu.VMEM((2, MOE_BM, D_MODEL), F32),
                            pltpu.VMEM((2, D_MODEL, D_EXPERT), F32), pltpu.VMEM((2, D_MODEL, D_EXPERT), F32),
                            pltpu.VMEM((2, D_EXPERT, D_MODEL), F32),
                            pltpu.SemaphoreType.DMA((2,)), pltpu.SemaphoreType.DMA((2,))],
        ),
        out_shape=jax.ShapeDtypeStruct((MOE_ROWS, D_MODEL), F32),
        compiler_params=pltpu.CompilerParams(dimension_semantics=("arbitrary",), vmem_limit_bytes=VMEM_LIMIT),
        name="moe_experts",
    )(block_e, plan, n_active, src, h2, w_gate, w_up, w_down)


def _combine_kernel(dest_ref, ys_ref, x1_ref, g1_ref, g2_ref, mod_ref, lng_ref, lnb_ref, o_ref, rows_sc, sem, *,
                    first_tile):
    base = (pl.program_id(0) + first_tile) * TM

    def start(t, carry):
        tok = base + t
        _row_copy(ys_ref, dest_ref[tok], rows_sc.at[0], t, sem).start()
        _row_copy(ys_ref, dest_ref[TOK + tok], rows_sc.at[1], t, sem).start()
        return carry

    def wait(t, carry):
        _row_copy(ys_ref, 0, rows_sc.at[0], 0, sem).wait()
        _row_copy(ys_ref, 0, rows_sc.at[1], 0, sem).wait()
        return carry

    lax.fori_loop(0, TM, start, 0)
    lax.fori_loop(0, TM, wait, 0)
    y = g1_ref[...] * rows_sc[0] + g2_ref[...] * rows_sc[1]
    z = DEEPNORM_ALPHA * x1_ref[...] + mod_ref[0, 5:6, :] * y
    o_ref[...] = _standardize(z) * lng_ref[...] + lnb_ref[...]


def _combine_call(dest, ys, x1, g1, g2, mod, ln_g, ln_b, *, first_tile):
    row_spec = lambda w: pl.BlockSpec((TM, w), lambda i, d: (i + first_tile, 0))
    vec_spec = pl.BlockSpec((1, D_MODEL), lambda i, d: (0, 0))
    return pl.pallas_call(
        functools.partial(_combine_kernel, first_tile=first_tile),
        grid_spec=pltpu.PrefetchScalarGridSpec(
            num_scalar_prefetch=1,
            grid=(N_TILES - first_tile,),
            in_specs=[pl.BlockSpec(memory_space=pl.ANY), row_spec(D_MODEL), row_spec(1), row_spec(1),
                      pl.BlockSpec((1, 6, D_MODEL), lambda i, d: (jnp.minimum(i + first_tile, 1), 0, 0)),
                      vec_spec, vec_spec],
            out_specs=pl.BlockSpec((TM, D_MODEL), lambda i, d: (i, 0)),
            scratch_shapes=[pltpu.VMEM((2, TM, D_MODEL), F32), pltpu.SemaphoreType.DMA(())],
        ),
        out_shape=jax.ShapeDtypeStruct((TOK - first_tile * TM, D_MODEL), F32),
        compiler_params=pltpu.CompilerParams(dimension_semantics=("arbitrary",), vmem_limit_bytes=VMEM_LIMIT),
        name="moe_combine",
    )(dest, ys, x1, g1, g2, mod, ln_g, ln_b)


def _rope_tables(dim):
    n_freq = dim // 4
    inv = ROPE_THETA ** (-jnp.arange(n_freq, dtype=F32) / n_freq)
    t = jnp.arange(SEQ)
    rows = (t // GRID_W).astype(F32)
    cols = (t % GRID_W).astype(F32)
    lane = jnp.arange(LANES)
    within = lane % dim
    use_col = within >= dim // 2
    freq = within % n_freq
    second_half = (within % (dim // 2)) >= n_freq
    pos = jnp.where(use_col[None, :], cols[:, None], rows[:, None])
    ang = pos * inv[freq][None, :]
    cos = jnp.cos(ang)
    sin = jnp.where(second_half[None, :], jnp.sin(ang), -jnp.sin(ang))
    cos = jnp.concatenate([jnp.ones((CTX, LANES), F32), cos], axis=0)
    sin = jnp.concatenate([jnp.zeros((CTX, LANES), F32), sin], axis=0)
    return cos, sin


def kernel(x, c, ctx, c_ctx, w_ada, b_ada, w_in, w_out, a_q_norm, a_k_norm, a_out_norm, b_lambda, b_out_norm,
           c_spatial, c_spatial_bias, c_out_norm, ln1_g, ln1_b, ln2_g, ln2_b, w_router, router_bias,
           w_gate, w_up, w_down):
    assert x.shape == (1, SEQ, D_MODEL) and ctx.shape == (1, CTX, D_MODEL)
    cos_a, sin_a = _rope_tables(HEAD_DIM)
    cos_b, sin_b = _rope_tables(B_QK_DIM)
    tabs = (cos_a, sin_a, cos_b, sin_b)

    c_rep = jnp.broadcast_to(jnp.stack([c_ctx, c[0]])[:, :, None], (2, D_MODEL, LANES))
    mods = _ada_call(c_rep, w_ada, b_ada).reshape(DEPTH, 2, 6, D_MODEL)

    w_rows = w_in.astype(BF16)
    w_t = _wt_call(w_in)
    tabs_t = tuple(t.T for t in tabs)
    col_rep = lambda a: jnp.broadcast_to(a[:, :, None], (DEPTH, HEAD_DIM, TM))
    gq_t, ga_t, gb_t = col_rep(a_q_norm), col_rep(a_out_norm), col_rep(b_out_norm)
    w_out_b = w_out.astype(BF16)
    ws_b = c_spatial.astype(BF16)
    bs_rep = jnp.broadcast_to(c_spatial_bias[:, :, :, None], (DEPTH, C_GROUPS, CHUNK, LANES))
    wr_pad = jnp.pad(w_router, ((0, 0), (0, LANES - N_EXPERTS)))
    wr_hi = wr_pad.astype(BF16)
    wr_t = jnp.stack([wr_hi, (wr_pad - wr_hi.astype(F32)).astype(BF16)])
    rb_rep = jnp.broadcast_to(router_bias[:, None], (N_EXPERTS, TM))
    tri = jnp.triu(jnp.ones((TM, TM), BF16), k=1)

    xs = jnp.concatenate([ctx[0], x[0]], axis=0)
    for l in range(DEPTH):
        lam_init = 0.8 - 0.6 * math.exp(-0.3 * l)
        mod = mods[l]
        vec = lambda a: a[l].reshape(1, -1)
        qar, qaf, ka, va, qbr, qbf, kb, vb, yc = _inproj_call(
            l, xs, mod, w_rows, w_t, tabs, tabs_t, gq_t[l], vec(a_k_norm), ws_b[l], bs_rep[l], vec(c_out_norm))
        ya = _attn_call(qar, qaf, ka, va, ga_t[l], None, groups=A_GROUP, kv_heads=1, diff=False,
                        lam_init=lam_init, name="attn_gqa")
        yb = _attn_call(qbr, qbf, kb, vb, gb_t[l], b_lambda[l], groups=4, kv_heads=2, diff=True,
                        lam_init=lam_init, name="attn_diff")
        x1, h2, route, counts = _outproj_call(l, ya, yb, yc, w_out_b, xs, mod, vec(ln1_g), vec(ln1_b),
                                              wr_t, rb_rep, tri)

        cnt = counts[:, 0].astype(jnp.int32)
        padded = (cnt + MOE_BM - 1) // MOE_BM * MOE_BM
        ends = jnp.cumsum(padded)
        starts = ends - padded
        e1 = route[0].astype(jnp.int32)
        e2 = route[1].astype(jnp.int32)
        dest = jnp.concatenate([starts[e1] + route[2].astype(jnp.int32), starts[e2] + route[3].astype(jnp.int32)])
        n_active = (ends[-1] // MOE_BM).astype(jnp.int32)
        blk = jnp.minimum(jnp.arange(MOE_NB, dtype=jnp.int32), n_active - 1)
        block_e = jnp.minimum(jnp.sum(ends[None, :] <= (blk * MOE_BM)[:, None], axis=1), N_EXPERTS - 1).astype(jnp.int32)

        first = (jnp.arange(MOE_NB) < n_active) & (block_e != jnp.concatenate([jnp.full((1,), -1, jnp.int32),
                                                                                 block_e[:-1]]))
        w_slot = (jnp.cumsum(first) - 1) % 2
        e_idx = jnp.arange(N_EXPERTS)
        later = jnp.where((cnt > 0)[None, :] & (e_idx[None, :] > e_idx[:, None]), e_idx[None, :], N_EXPERTS).min(axis=1)
        next_e = jnp.where(later == N_EXPERTS, -1, later)[block_e]
        plan = jnp.stack([first.astype(jnp.int32), w_slot.astype(jnp.int32), next_e.astype(jnp.int32)],
                         axis=1).reshape(-1)

        src = _slot_source_call(dest)
        ys = _moe_call(l, block_e, plan, n_active.reshape(1), src, h2, w_gate, w_up, w_down)
        xs = _combine_call(dest, ys, x1, route[4].reshape(TOK, 1), route[5].reshape(TOK, 1), mod,
                           vec(ln2_g), vec(ln2_b), first_tile=1 if l == DEPTH - 1 else 0)
    return xs.reshape(1, SEQ, D_MODEL)
```

```python
import functools
import math

import jax
import jax.numpy as jnp
from jax import lax
from jax.experimental import pallas as pl
from jax.experimental.pallas import tpu as pltpu

F32 = jnp.float32
BF16 = jnp.bfloat16

D_MODEL = 2048
SEQ = 8192
CTX = 256
TOK = CTX + SEQ
DEPTH = 4
GRID_W = 64

HEAD_DIM = 128
A_HEADS = 8
A_KV_HEADS = 2
A_GROUP = A_HEADS // A_KV_HEADS
B_HEADS = 4
B_QK_DIM = 64
C_GROUPS = 4
CHUNK = 128
ROPE_THETA = 10000.0

A_Q = A_HEADS * HEAD_DIM
A_KV = A_KV_HEADS * HEAD_DIM
B_QK = B_HEADS * HEAD_DIM
B_V = B_HEADS * HEAD_DIM
C_U = C_GROUPS * HEAD_DIM
D_IN = A_Q + 2 * A_KV + 2 * B_QK + B_V + 2 * C_U
D_MIX = A_Q + B_V + C_U
OFF_AQ = 0
OFF_AK = OFF_AQ + A_Q
OFF_AV = OFF_AK + A_KV
OFF_BQ = OFF_AV + A_KV
OFF_BK = OFF_BQ + B_QK
OFF_BV = OFF_BK + B_QK
OFF_CU = OFF_BV + B_V
OFF_CV = OFF_CU + C_U

N_EXPERTS = 16
N_EXPERT_GROUPS = 4
EXPERTS_PER_GROUP = 4
TOP_K = 2
D_EXPERT = 1024

DEEPNORM_ALPHA = (2 * DEPTH) ** 0.25
NORM_EPS = 1e-6
LOG2_E = math.log2(math.e)

LANES = 128
VMEM_LIMIT = 60 * 1024 * 1024

TM = 256
N_TILES = TOK // TM
ADA_TN = 512
LN_ROWS = 16
ATT_CK = 512
ATT_UNROLL = 2
ATT_TQ = 256
MOE_BM = 256
MOE_NB = (TOK * TOP_K + N_EXPERTS * (MOE_BM - 1) + MOE_BM - 1) // MOE_BM
MOE_ROWS = MOE_NB * MOE_BM

NT_DIMS = (((1,), (1,)), ((), ()))


def _rms(x, g):
    return x * lax.rsqrt(jnp.mean(x * x, axis=-1, keepdims=True) + NORM_EPS) * g


def _gelu(x):
    return 0.5 * x * (1.0 + lax.erf(x * (2.0 ** -0.5)))


def _standardize(x):
    mu = jnp.mean(x, axis=-1, keepdims=True)
    xc = x - mu
    var = jnp.mean(xc * xc, axis=-1, keepdims=True)
    return xc * lax.rsqrt(var + NORM_EPS)


def _ada_kernel(c_ref, w_ref, b_ref, o_ref):
    for r in range(2):
        cv = c_ref[r]
        act = cv * jax.nn.sigmoid(cv)
        for j in range(ADA_TN // LANES):
            cols = slice(j * LANES, (j + 1) * LANES)
            o_ref[0, r:r + 1, cols] = jnp.sum(w_ref[0, :, cols] * act, axis=0, keepdims=True) + b_ref[0, :, cols]


def _ada_call(c_rep, w_ada, b_ada):
    n_out = w_ada.shape[-1]
    return pl.pallas_call(
        _ada_kernel,
        grid=(DEPTH, n_out // ADA_TN),
        in_specs=[
            pl.BlockSpec((2, D_MODEL, LANES), lambda l, n: (0, 0, 0)),
            pl.BlockSpec((1, D_MODEL, ADA_TN), lambda l, n: (l, 0, n)),
            pl.BlockSpec((1, 1, ADA_TN), lambda l, n: (l, 0, n)),
        ],
        out_specs=pl.BlockSpec((1, 2, ADA_TN), lambda l, n: (l, 0, n)),
        out_shape=jax.ShapeDtypeStruct((DEPTH, 2, n_out), F32),
        compiler_params=pltpu.CompilerParams(dimension_semantics=("arbitrary", "arbitrary"),
                                             vmem_limit_bytes=VMEM_LIMIT),
        name="ada_ln",
    )(c_rep, w_ada, b_ada.reshape(DEPTH, 1, n_out))


def _rope(x, cos, sin_signed, half, lane):
    fwd = pltpu.roll(x, LANES - half, 1)
    bwd = pltpu.roll(x, half, 1)
    partner = jnp.where((lane & (2 * half - 1)) < half, fwd, bwd)
    return x * cos + partner * sin_signed


def _rope_t(x, cos, sin_signed, half):
    blocks = [x[b * half:(b + 1) * half] for b in range(HEAD_DIM // half)]
    partner = jnp.concatenate([blocks[b ^ 1] for b in range(len(blocks))], axis=0)
    return x * cos + partner * sin_signed


T_QA = 0
T_QB = T_QA + A_Q
T_VA = T_QB + B_QK
T_VB = T_VA + A_KV
T_ROWS = T_VB + B_V
R_KA, R_KB, R_CU, R_CV = OFF_AK, OFF_BK, OFF_CU, OFF_CV
T_SEG = 512
WT_BLK = 256
WT_SRC_BLOCKS = tuple(c // WT_BLK for off, n in ((OFF_AQ, A_Q), (OFF_BQ, B_QK), (OFF_AV, A_KV), (OFF_BV, B_V))
                      for c in range(off, off + n, WT_BLK))


def _wt_kernel(blk_ref, w_ref, o_ref):
    del blk_ref
    o_ref[0] = w_ref[0].T.astype(BF16)


def _wt_call(w_in):
    blocks = jnp.asarray(WT_SRC_BLOCKS, jnp.int32)
    return pl.pallas_call(
        _wt_kernel,
        grid_spec=pltpu.PrefetchScalarGridSpec(
            num_scalar_prefetch=1,
            grid=(DEPTH, len(WT_SRC_BLOCKS)),
            in_specs=[pl.BlockSpec((1, D_MODEL, WT_BLK), lambda l, j, blk: (l, 0, blk[j]))],
            out_specs=pl.BlockSpec((1, WT_BLK, D_MODEL), lambda l, j, blk: (l, j, 0)),
        ),
        out_shape=jax.ShapeDtypeStruct((DEPTH, T_ROWS, D_MODEL), BF16),
        compiler_params=pltpu.CompilerParams(dimension_semantics=("arbitrary", "arbitrary")),
        name="w_in_transpose",
    )(blocks, w_in)


def _inproj_kernel(x_ref, mod_ref, wr_ref, wt_ref, cosa_ref, sina_ref, cosb_ref, sinb_ref,
                   cosat_ref, sinat_ref, cosbt_ref, sinbt_ref, gqt_ref, gk_ref, ws_ref, bs_ref, gc_ref,
                   qar_ref, qaf_ref, ka_ref, va_ref, qbr_ref, qbf_ref, kb_ref, vb_ref, yc_ref, pt_sc):
    x = x_ref[...]
    h = (x * (1.0 + mod_ref[0, 1:2, :]) + mod_ref[0, 0:1, :]).astype(BF16)
    lane = lax.broadcasted_iota(jnp.int32, (TM, LANES), 1)
    cosa, sina = cosa_ref[...], sina_ref[...]
    cosb, sinb = cosb_ref[...], sinb_ref[...]
    scale_a = HEAD_DIM ** -0.5 * LOG2_E
    scale_b = B_QK_DIM ** -0.5 * LOG2_E

    for r0 in range(0, T_ROWS, T_SEG):
        n = min(T_SEG, T_ROWS - r0)
        pt_sc[r0:r0 + n, :] = lax.dot_general(wt_ref[0, r0:r0 + n, :], h, NT_DIMS, preferred_element_type=F32)
    for j in range(A_HEADS):
        q = pt_sc[T_QA + j * HEAD_DIM:T_QA + (j + 1) * HEAD_DIM, :]
        qn = q * lax.rsqrt(jnp.mean(q * q, axis=0, keepdims=True) + NORM_EPS) * (gqt_ref[...] * scale_a)
        qaf_ref[j] = qn.astype(BF16)
        qar_ref[j] = _rope_t(qn, cosat_ref[...], sinat_ref[...], 32).astype(BF16)
    zeros_half = jnp.zeros((B_QK_DIM, TM), BF16)
    for j in range(B_HEADS):
        qs = pt_sc[T_QB + j * HEAD_DIM:T_QB + (j + 1) * HEAD_DIM, :] * scale_b
        for q, dst in ((qs, qbf_ref), (_rope_t(qs, cosbt_ref[...], sinbt_ref[...], 16), qbr_ref)):
            qb = q.astype(BF16)
            dst[2 * j] = jnp.concatenate([qb[:B_QK_DIM], zeros_half], axis=0)
            dst[2 * j + 1] = jnp.concatenate([zeros_half, qb[B_QK_DIM:]], axis=0)
    for j in range(A_KV_HEADS):
        va_ref[j] = pt_sc[T_VA + j * HEAD_DIM:T_VA + (j + 1) * HEAD_DIM, :].astype(BF16)
    for j in range(B_HEADS):
        vb_ref[j] = pt_sc[T_VB + j * HEAD_DIM:T_VB + (j + 1) * HEAD_DIM, :].astype(BF16)

    def proj(col):
        p = jnp.dot(h, wr_ref[0, :, col:col + 2 * LANES], preferred_element_type=F32)
        return p[:, :LANES], p[:, LANES:]

    for j, k in enumerate(proj(R_KA)):
        ka_ref[j] = _rope(_rms(k, gk_ref[...]), cosa, sina, 32, lane).astype(BF16)
    for j2 in range(B_HEADS // 2):
        for j, k in zip((2 * j2, 2 * j2 + 1), proj(R_KB + j2 * 2 * LANES)):
            kb_ref[j] = _rope(k, cosb, sinb, 16, lane).astype(BF16)
    for j2 in range(C_GROUPS // 2):
        us = proj(R_CU + j2 * 2 * LANES)
        vs = proj(R_CV + j2 * 2 * LANES)
        for g, u, v in zip((2 * j2, 2 * j2 + 1), us, vs):
            u = _gelu(u)
            v = _standardize(_gelu(v)).astype(BF16)
            for c in range(TM // CHUNK):
                rows = slice(c * CHUNK, (c + 1) * CHUNK)
                mixed = jnp.dot(ws_ref[g], v[rows], preferred_element_type=F32) + bs_ref[g]
                yc_ref[rows, g * LANES:(g + 1) * LANES] = _rms(u[rows] * mixed, gc_ref[...]).astype(BF16)


def _inproj_call(layer, xs, mod, w_rows, w_t, tabs, tabs_t, gq_t, gk, ws, bs, gc):
    def heads(n):
        return (jax.ShapeDtypeStruct((n, TOK, HEAD_DIM), BF16),
                pl.BlockSpec((n, TM, HEAD_DIM), lambda i: (0, i, 0)))

    def heads_t(n):
        return (jax.ShapeDtypeStruct((n, HEAD_DIM, TOK), BF16),
                pl.BlockSpec((n, HEAD_DIM, TM), lambda i: (0, 0, i)))

    outs = [heads_t(A_HEADS), heads_t(A_HEADS), heads(A_KV_HEADS), heads_t(A_KV_HEADS),
            heads_t(2 * B_HEADS), heads_t(2 * B_HEADS), heads(B_HEADS), heads_t(B_HEADS),
            (jax.ShapeDtypeStruct((TOK, C_U), BF16), pl.BlockSpec((TM, C_U), lambda i: (i, 0)))]
    tab_spec = pl.BlockSpec((TM, LANES), lambda i: (i, 0))
    tab_t_spec = pl.BlockSpec((HEAD_DIM, TM), lambda i: (0, i))
    vec_spec = pl.BlockSpec((1, LANES), lambda i: (0, 0))
    return pl.pallas_call(
        _inproj_kernel,
        grid=(N_TILES,),
        in_specs=[
            pl.BlockSpec((TM, D_MODEL), lambda i: (i, 0)),
            pl.BlockSpec((1, 6, D_MODEL), lambda i: (jnp.minimum(i, 1), 0, 0)),
            pl.BlockSpec((1, D_MODEL, D_IN), lambda i: (layer, 0, 0), pipeline_mode=pl.Buffered(1)),
            pl.BlockSpec((1, T_ROWS, D_MODEL), lambda i: (layer, 0, 0), pipeline_mode=pl.Buffered(1)),
            tab_spec, tab_spec, tab_spec, tab_spec,
            tab_t_spec, tab_t_spec, tab_t_spec, tab_t_spec,
            pl.BlockSpec((HEAD_DIM, TM), lambda i: (0, 0)),
            vec_spec,
            pl.BlockSpec((C_GROUPS, CHUNK, CHUNK), lambda i: (0, 0, 0)),
            pl.BlockSpec((C_GROUPS, CHUNK, LANES), lambda i: (0, 0, 0)),
            vec_spec,
        ],
        out_specs=[o[1] for o in outs],
        out_shape=[o[0] for o in outs],
        scratch_shapes=[pltpu.VMEM((T_ROWS, TM), F32)],
        compiler_params=pltpu.CompilerParams(dimension_semantics=("arbitrary",), vmem_limit_bytes=VMEM_LIMIT),
        name="in_proj",
    )(xs, mod, w_rows, w_t, *tabs, *tabs_t, gq_t, gk, ws, bs, gc)


def _attn_kernel(*refs, groups, kv_heads, diff, lam_init):
    if diff:
        qr_ref, qf_ref, k_ref, vt_ref, lam_ref, g_ref, o_ref, m_sc, l_sc, acc_sc, s_sc, p_sc, a_sc, c_sc = refs
    else:
        qr_ref, qf_ref, k_ref, vt_ref, g_ref, o_ref, m_sc, l_sc, acc_sc, s_sc, p_sc, a_sc, c_sc = refs
    i = pl.program_id(1)
    n_chunks = SEQ // ATT_CK
    kv_of = lambda g: g // (groups // kv_heads)

    for g in range(groups):
        s = jnp.dot(k_ref[kv_of(g), 0:CTX, :], qf_ref[g], preferred_element_type=F32)
        m0 = jnp.max(s, axis=0, keepdims=True)
        p = jnp.exp2(s - m0)
        m_sc[g] = m0
        l_sc[g] = jnp.sum(p, axis=0, keepdims=True)
        acc_sc[g] = jnp.zeros((HEAD_DIM, ATT_TQ), F32)
        a_sc[2 * g + 1] = jnp.ones((1, ATT_TQ), F32)
        p_sc[2 * g + 1, 0:CTX, :] = p.astype(BF16)
        p_sc[2 * g + 1, CTX:ATT_CK, :] = jnp.zeros((ATT_CK - CTX, ATT_TQ), BF16)

    def chunk_off(c):
        if isinstance(c, int):
            return 0 if c < 0 else CTX + min(c, n_chunks - 1) * ATT_CK
        latent = CTX + jnp.minimum(c, n_chunks - 1) * ATT_CK
        return pl.multiple_of(jnp.where(c < 0, 0, latent), CTX)

    def scores(g, c, slot):
        s = jnp.dot(k_ref[kv_of(g), pl.ds(chunk_off(c), ATT_CK), :], qr_ref[g], preferred_element_type=F32)
        s_sc[2 * g + slot] = s
        c_sc[2 * g + slot] = jnp.max(s, axis=0, keepdims=True)

    def softmax(g, slot):
        m_prev = m_sc[g]
        m_new = jnp.maximum(m_prev, c_sc[2 * g + slot])
        alpha = jnp.exp2(m_prev - m_new)
        p = jnp.exp2(s_sc[2 * g + slot] - m_new)
        l_sc[g] = alpha * l_sc[g] + jnp.sum(p, axis=0, keepdims=True)
        m_sc[g] = m_new
        a_sc[2 * g + slot] = alpha
        p_sc[2 * g + slot] = p.astype(BF16)

    def values(g, c, slot):
        acc_sc[g] = a_sc[2 * g + slot] * acc_sc[g] + jnp.dot(vt_ref[kv_of(g), :, pl.ds(chunk_off(c), ATT_CK)],
                                                             p_sc[2 * g + slot], preferred_element_type=F32)

    @pl.when(i < CTX // ATT_TQ)
    def _context_queries():
        for g in range(groups):
            values(g, -1, 1)

    @pl.when(i >= CTX // ATT_TQ)
    def _latent_keys():
        for g in range(groups):
            scores(g, 0, 0)

        def body(t, carry):
            for u in range(ATT_UNROLL):
                c0 = 2 * (ATT_UNROLL * t + u)
                for g in range(groups):
                    scores(g, c0 + 1, 1)
                    softmax(g, 0)
                    values(g, c0 - 1, 1)
                for g in range(groups):
                    scores(g, c0 + 2, 0)
                    softmax(g, 1)
                    values(g, c0, 0)
            return carry

        lax.fori_loop(0, n_chunks // (2 * ATT_UNROLL), body, 0)
        for g in range(groups):
            values(g, n_chunks - 1, 1)

    def finish(o_t, gain):
        o_t = o_t * lax.rsqrt(jnp.mean(o_t * o_t, axis=0, keepdims=True) + NORM_EPS) * gain
        return o_t.T.astype(BF16)

    if diff:
        lam = lam_ref[...]
        lam_val = (jnp.exp(jnp.sum(lam[0:1] * lam[1:2], axis=-1, keepdims=True))
                   - jnp.exp(jnp.sum(lam[2:3] * lam[3:4], axis=-1, keepdims=True)) + lam_init)
        for h in range(kv_heads):
            d = acc_sc[2 * h] / l_sc[2 * h] - lam_val * (acc_sc[2 * h + 1] / l_sc[2 * h + 1])
            o_ref[:, h * HEAD_DIM:(h + 1) * HEAD_DIM] = finish(d, g_ref[...] * (1.0 - lam_init))
    else:
        for g in range(groups):
            o_ref[:, g * HEAD_DIM:(g + 1) * HEAD_DIM] = finish(acc_sc[g] / l_sc[g], g_ref[...])


def _attn_call(qr, qf, k, vt, g_out_t, lam, *, groups, kv_heads, diff, lam_init, name):
    n_steps = k.shape[0] // kv_heads
    k_spec = pl.BlockSpec((kv_heads, TOK, HEAD_DIM), lambda h, i: (h, 0, 0))
    vt_spec = pl.BlockSpec((kv_heads, HEAD_DIM, TOK), lambda h, i: (h, 0, 0))
    q_spec = pl.BlockSpec((groups, HEAD_DIM, ATT_TQ), lambda h, i: (h, 0, i))
    in_specs = [q_spec, q_spec, k_spec, vt_spec]
    args = [qr, qf, k, vt]
    if diff:
        in_specs.append(pl.BlockSpec((4, B_QK_DIM), lambda h, i: (0, 0)))
        args.append(lam)
        out_w = kv_heads * HEAD_DIM
    else:
        out_w = groups * HEAD_DIM
    in_specs.append(pl.BlockSpec((HEAD_DIM, ATT_TQ), lambda h, i: (0, 0)))
    args.append(g_out_t)
    return pl.pallas_call(
        functools.partial(_attn_kernel, groups=groups, kv_heads=kv_heads, diff=diff, lam_init=lam_init),
        grid=(n_steps, TOK // ATT_TQ),
        in_specs=in_specs,
        out_specs=pl.BlockSpec((ATT_TQ, out_w), lambda h, i: (i, h)),
        out_shape=jax.ShapeDtypeStruct((TOK, n_steps * out_w), BF16),
        scratch_shapes=[pltpu.VMEM((groups, 1, ATT_TQ), F32), pltpu.VMEM((groups, 1, ATT_TQ), F32),
                        pltpu.VMEM((groups, HEAD_DIM, ATT_TQ), F32),
                        pltpu.VMEM((2 * groups, ATT_CK, ATT_TQ), F32),
                        pltpu.VMEM((2 * groups, ATT_CK, ATT_TQ), BF16),
                        pltpu.VMEM((2 * groups, 1, ATT_TQ), F32),
                        pltpu.VMEM((2 * groups, 1, ATT_TQ), F32)],
        compiler_params=pltpu.CompilerParams(dimension_semantics=("arbitrary", "arbitrary"),
                                             vmem_limit_bytes=VMEM_LIMIT),
        name=name,
    )(*args)


def _pair_max(vals):
    best = None
    for a in range(len(vals)):
        for b in range(a + 1, len(vals)):
            s = vals[a] + vals[b]
            best = s if best is None else jnp.maximum(best, s)
    return best


def _outproj_kernel(ya_ref, yb_ref, yc_ref, w_ref, x_ref, mod_ref, lng_ref, lnb_ref, wr_ref, rb_ref, tri_ref,
                    x1_ref, h2_ref, route_ref, cnt_ref, carry_sc, mix_sc, y_sc, hi_sc, lo_sc):
    i = pl.program_id(0)

    @pl.when(i == 0)
    def _init():
        carry_sc[...] = jnp.zeros_like(carry_sc)

    mix_sc[:, 0:A_Q] = ya_ref[...]
    mix_sc[:, A_Q:A_Q + B_V] = yb_ref[...]
    mix_sc[:, A_Q + B_V:D_MIX] = yc_ref[...]
    y_sc[...] = jnp.dot(mix_sc[...], w_ref[0], preferred_element_type=F32)

    def rows_step(r, carry):
        rows = pl.ds(pl.multiple_of(r * LN_ROWS, LN_ROWS), LN_ROWS)
        z = DEEPNORM_ALPHA * x_ref[rows, :] + mod_ref[0, 2:3, :] * y_sc[rows, :]
        x1 = _standardize(z) * lng_ref[...] + lnb_ref[...]
        x1_ref[rows, :] = x1
        h2 = x1 * (1.0 + mod_ref[0, 4:5, :]) + mod_ref[0, 3:4, :]
        h2_ref[rows, :] = h2
        hi = h2.astype(BF16)
        hi_sc[rows, :] = hi
        lo_sc[rows, :] = (h2 - hi.astype(F32)).astype(BF16)
        return carry

    lax.fori_loop(0, TM // LN_ROWS, rows_step, 0, unroll=2)

    lg = (jnp.dot(hi_sc[...], wr_ref[0], preferred_element_type=F32)
          + jnp.dot(hi_sc[...], wr_ref[1], preferred_element_type=F32)
          + jnp.dot(lo_sc[...], wr_ref[0], preferred_element_type=F32))
    logits = lg.T[0:N_EXPERTS, :]
    scores = jax.nn.sigmoid(logits)
    biased = scores + rb_ref[...]
    b_rows = [biased[e:e + 1, :] for e in range(N_EXPERTS)]
    s_rows = [scores[e:e + 1, :] for e in range(N_EXPERTS)]
    group_score = [_pair_max(b_rows[g * EXPERTS_PER_GROUP:(g + 1) * EXPERTS_PER_GROUP])
                   for g in range(N_EXPERT_GROUPS)]
    best = group_score[0]
    grp = jnp.zeros_like(best)
    for g in range(1, N_EXPERT_GROUPS):
        better = group_score[g] > best
        grp = jnp.where(better, float(g), grp)
        best = jnp.where(better, group_score[g], best)

    def pick(rows_, j):
        out = rows_[j]
        for g in range(1, N_EXPERT_GROUPS):
            out = jnp.where(grp == float(g), rows_[g * EXPERTS_PER_GROUP + j], out)
        return out

    vb = [pick(b_rows, j) for j in range(EXPERTS_PER_GROUP)]
    vs = [pick(s_rows, j) for j in range(EXPERTS_PER_GROUP)]
    chosen = []
    for j in range(EXPERTS_PER_GROUP):
        rank = jnp.zeros_like(best)
        for k in range(EXPERTS_PER_GROUP):
            if k == j:
                continue
            ahead = (vb[k] > vb[j]) | ((vb[k] == vb[j]) if k < j else False)
            rank = rank + jnp.where(ahead, 1.0, 0.0)
        chosen.append(rank < float(TOP_K))
    loc1 = jnp.full_like(best, float(EXPERTS_PER_GROUP))
    loc2 = jnp.full_like(best, -1.0)
    for j in range(EXPERTS_PER_GROUP):
        loc1 = jnp.where(chosen[j], jnp.minimum(loc1, float(j)), loc1)
        loc2 = jnp.where(chosen[j], jnp.maximum(loc2, float(j)), loc2)
    g1 = jnp.zeros_like(best)
    g2 = jnp.zeros_like(best)
    for j in range(EXPERTS_PER_GROUP):
        g1 = jnp.where(loc1 == float(j), vs[j], g1)
        g2 = jnp.where(loc2 == float(j), vs[j], g2)
    gsum = g1 + g2
    e1 = grp * float(EXPERTS_PER_GROUP) + loc1
    e2 = grp * float(EXPERTS_PER_GROUP) + loc2

    eidx = lax.broadcasted_iota(jnp.int32, (N_EXPERTS, TM), 0).astype(F32)
    is1 = eidx == e1
    is2 = eidx == e2
    sel = jnp.where(is1 | is2, 1.0, 0.0)
    before = jnp.dot(sel.astype(BF16), tri_ref[...], preferred_element_type=F32) + carry_sc[...]
    pos1 = jnp.sum(jnp.where(is1, before, 0.0), axis=0, keepdims=True)
    pos2 = jnp.sum(jnp.where(is2, before, 0.0), axis=0, keepdims=True)
    carry = carry_sc[...] + jnp.sum(sel, axis=1, keepdims=True)
    carry_sc[...] = carry
    cnt_ref[...] = carry[:, :LANES]

    route_ref[0:1, :] = e1
    route_ref[1:2, :] = e2
    route_ref[2:3, :] = pos1
    route_ref[3:4, :] = pos2
    route_ref[4:5, :] = g1 / gsum
    route_ref[5:6, :] = g2 / gsum
    route_ref[6:8, :] = jnp.zeros((2, TM), F32)


def _outproj_call(layer, ya, yb, yc, w_out, xs, mod, ln_g, ln_b, wr_t, rb_rep, tri):
    row_spec = lambda w: pl.BlockSpec((TM, w), lambda i: (i, 0))
    vec_spec = pl.BlockSpec((1, D_MODEL), lambda i: (0, 0))
    return pl.pallas_call(
        _outproj_kernel,
        grid=(N_TILES,),
        in_specs=[
            row_spec(A_Q), row_spec(B_V), row_spec(C_U),
            pl.BlockSpec((1, D_MIX, D_MODEL), lambda i: (layer, 0, 0), pipeline_mode=pl.Buffered(1)),
            row_spec(D_MODEL),
            pl.BlockSpec((1, 6, D_MODEL), lambda i: (jnp.minimum(i, 1), 0, 0)),
            vec_spec, vec_spec,
            pl.BlockSpec((2, D_MODEL, LANES), lambda i: (0, 0, 0)),
            pl.BlockSpec((N_EXPERTS, TM), lambda i: (0, 0)),
            pl.BlockSpec((TM, TM), lambda i: (0, 0)),
        ],
        out_specs=[row_spec(D_MODEL), row_spec(D_MODEL),
                   pl.BlockSpec((8, TM), lambda i: (0, i)),
                   pl.BlockSpec((N_EXPERTS, LANES), lambda i: (0, 0))],
        out_shape=[jax.ShapeDtypeStruct((TOK, D_MODEL), F32), jax.ShapeDtypeStruct((TOK, D_MODEL), F32),
                   jax.ShapeDtypeStruct((8, TOK), F32), jax.ShapeDtypeStruct((N_EXPERTS, LANES), F32)],
        scratch_shapes=[pltpu.VMEM((N_EXPERTS, TM), F32), pltpu.VMEM((TM, D_MIX), BF16),
                        pltpu.VMEM((TM, D_MODEL), F32), pltpu.VMEM((TM, D_MODEL), BF16),
                        pltpu.VMEM((TM, D_MODEL), BF16)],
        compiler_params=pltpu.CompilerParams(dimension_semantics=("arbitrary",), vmem_limit_bytes=VMEM_LIMIT),
        name="out_proj",
    )(ya, yb, yc, w_out, xs, mod, ln_g, ln_b, wr_t, rb_rep, tri)


def _row_copy(src_ref, src_row, dst_ref, dst_row, sem):
    return pltpu.make_async_copy(src_ref.at[pl.ds(src_row, 1)], dst_ref.at[pl.ds(dst_row, 1)], sem)


def _slot_source_kernel(dest_ref, src_ref):
    def clear(s, carry):
        src_ref[s] = 0
        return carry

    def scatter(t, carry):
        src_ref[dest_ref[t]] = t
        src_ref[dest_ref[TOK + t]] = t
        return carry

    lax.fori_loop(0, MOE_ROWS, clear, 0, unroll=8)
    lax.fori_loop(0, TOK, scatter, 0, unroll=8)


def _slot_source_call(dest):
    return pl.pallas_call(
        _slot_source_kernel,
        in_specs=[pl.BlockSpec(memory_space=pltpu.SMEM)],
        out_specs=pl.BlockSpec(memory_space=pltpu.SMEM),
        out_shape=jax.ShapeDtypeStruct((MOE_ROWS,), jnp.int32),
        name="moe_slot_source",
    )(dest)


def _moe_kernel(be_ref, plan_ref, na_ref, src_ref, h_ref, wg_ref, wu_ref, wd_ref, o_ref,
                x_sc, wg_sc, wu_sc, wd_sc, sems, wsems, *, layer):
    b = pl.program_id(0)
    n_active = na_ref[0]

    def start_gather(block, slot):
        for r in range(MOE_BM):
            _row_copy(h_ref, src_ref[block * MOE_BM + r], x_sc.at[slot], r, sems.at[slot]).start()

    def wait_gather(slot):
        for r in range(MOE_BM):
            _row_copy(h_ref, 0, x_sc.at[slot], 0, sems.at[slot]).wait()

    def weight_copies(expert, slot):
        return (pltpu.make_async_copy(wg_ref.at[layer, expert], wg_sc.at[slot], wsems.at[slot]),
                pltpu.make_async_copy(wu_ref.at[layer, expert], wu_sc.at[slot], wsems.at[slot]),
                pltpu.make_async_copy(wd_ref.at[layer, expert], wd_sc.at[slot], wsems.at[slot]))

    @pl.when(b == 0)
    def _first():
        for cp in weight_copies(be_ref[0], 0):
            cp.start()
        start_gather(0, 0)

    @pl.when(b + 1 < n_active)
    def _prefetch_next():
        start_gather(b + 1, (b + 1) % 2)

    @pl.when(b >= n_active)
    def _unused_block():
        o_ref[...] = jnp.zeros_like(o_ref)

    new_expert = (b < n_active) & (plan_ref[3 * b] == 1)
    wslot = plan_ref[3 * b + 1]
    next_expert = plan_ref[3 * b + 2]

    @pl.when(new_expert & (next_expert >= 0))
    def _fetch_next_expert():
        for cp in weight_copies(next_expert, 1 - wslot):
            cp.start()

    @pl.when(new_expert)
    def _await_expert():
        for cp in weight_copies(be_ref[b], wslot):
            cp.wait()

    @pl.when(b < n_active)
    def _block():
        slot = b % 2
        wait_gather(slot)
        xb = x_sc[slot]
        gate = jnp.dot(xb, wg_sc[wslot], preferred_element_type=F32)
        up = jnp.dot(xb, wu_sc[wslot], preferred_element_type=F32)
        act = gate * jax.nn.sigmoid(gate) * up
        o_ref[...] = jnp.dot(act, wd_sc[wslot], preferred_element_type=F32)


def _moe_call(layer, block_e, plan, n_active, src, h2, w_gate, w_up, w_down):
    return pl.pallas_call(
        functools.partial(_moe_kernel, layer=layer),
        grid_spec=pltpu.PrefetchScalarGridSpec(
            num_scalar_prefetch=4,
            grid=(MOE_NB,),
            in_specs=[pl.BlockSpec(memory_space=pl.ANY)] * 4,
            out_specs=pl.BlockSpec((MOE_BM, D_MODEL), lambda b, be, pn, na, sr: (b, 0)),
            scratch_shapes=[pltpu.VMEM((2, MOE_BM, D_MODEL), F32),
                            pltpu.VMEM((2, D_MODEL, D_EXPERT), F32), pltpu.VMEM((2, D_MODEL, D_EXPERT), F32),
                            pltpu.VMEM((2, D_EXPERT, D_MODEL), F32),
                            pltpu.SemaphoreType.DMA((2,)), pltpu.SemaphoreType.DMA((2,))],
        ),
        out_shape=jax.ShapeDtypeStruct((MOE_ROWS, D_MODEL), F32),
        compiler_params=pltpu.CompilerParams(dimension_semantics=("arbitrary",), vmem_limit_bytes=VMEM_LIMIT),
        name="moe_experts",
    )(block_e, plan, n_active, src, h2, w_gate, w_up, w_down)


def _combine_kernel(dest_ref, ys_ref, x1_ref, g1_ref, g2_ref, mod_ref, lng_ref, lnb_ref, o_ref, rows_sc, sem, *,
                    first_tile):
    base = (pl.program_id(0) + first_tile) * TM

    for t in range(TM):
        _row_copy(ys_ref, dest_ref[base + t], rows_sc.at[0], t, sem).start()
        _row_copy(ys_ref, dest_ref[TOK + base + t], rows_sc.at[1], t, sem).start()
    for t in range(TM):
        _row_copy(ys_ref, 0, rows_sc.at[0], 0, sem).wait()
        _row_copy(ys_ref, 0, rows_sc.at[1], 0, sem).wait()
    y = g1_ref[...] * rows_sc[0] + g2_ref[...] * rows_sc[1]
    z = DEEPNORM_ALPHA * x1_ref[...] + mod_ref[0, 5:6, :] * y
    o_ref[...] = _standardize(z) * lng_ref[...] + lnb_ref[...]


def _combine_call(dest, ys, x1, g1, g2, mod, ln_g, ln_b, *, first_tile):
    row_spec = lambda w: pl.BlockSpec((TM, w), lambda i, d: (i + first_tile, 0))
    vec_spec = pl.BlockSpec((1, D_MODEL), lambda i, d: (0, 0))
    return pl.pallas_call(
        functools.partial(_combine_kernel, first_tile=first_tile),
        grid_spec=pltpu.PrefetchScalarGridSpec(
            num_scalar_prefetch=1,
            grid=(N_TILES - first_tile,),
            in_specs=[pl.BlockSpec(memory_space=pl.ANY), row_spec(D_MODEL), row_spec(1), row_spec(1),
                      pl.BlockSpec((1, 6, D_MODEL), lambda i, d: (jnp.minimum(i + first_tile, 1), 0, 0)),
                      vec_spec, vec_spec],
            out_specs=pl.BlockSpec((TM, D_MODEL), lambda i, d: (i, 0)),
            scratch_shapes=[pltpu.VMEM((2, TM, D_MODEL), F32), pltpu.SemaphoreType.DMA(())],
        ),
        out_shape=jax.ShapeDtypeStruct((TOK - first_tile * TM, D_MODEL), F32),
        compiler_params=pltpu.CompilerParams(dimension_semantics=("arbitrary",), vmem_limit_bytes=VMEM_LIMIT),
        name="moe_combine",
    )(dest, ys, x1, g1, g2, mod, ln_g, ln_b)


def _rope_tables(dim):
    n_freq = dim // 4
    inv = ROPE_THETA ** (-jnp.arange(n_freq, dtype=F32) / n_freq)
    t = jnp.arange(SEQ)
    rows = (t // GRID_W).astype(F32)
    cols = (t % GRID_W).astype(F32)
    lane = jnp.arange(LANES)
    within = lane % dim
    use_col = within >= dim // 2
    freq = within % n_freq
    second_half = (within % (dim // 2)) >= n_freq
    pos = jnp.where(use_col[None, :], cols[:, None], rows[:, None])
    ang = pos * inv[freq][None, :]
    cos = jnp.cos(ang)
    sin = jnp.where(second_half[None, :], jnp.sin(ang), -jnp.sin(ang))
    cos = jnp.concatenate([jnp.ones((CTX, LANES), F32), cos], axis=0)
    sin = jnp.concatenate([jnp.zeros((CTX, LANES), F32), sin], axis=0)
    return cos, sin


def kernel(x, c, ctx, c_ctx, w_ada, b_ada, w_in, w_out, a_q_norm, a_k_norm, a_out_norm, b_lambda, b_out_norm,
           c_spatial, c_spatial_bias, c_out_norm, ln1_g, ln1_b, ln2_g, ln2_b, w_router, router_bias,
           w_gate, w_up, w_down):
    assert x.shape == (1, SEQ, D_MODEL) and ctx.shape == (1, CTX, D_MODEL)
    cos_a, sin_a = _rope_tables(HEAD_DIM)
    cos_b, sin_b = _rope_tables(B_QK_DIM)
    tabs = (cos_a, sin_a, cos_b, sin_b)

    c_rep = jnp.broadcast_to(jnp.stack([c_ctx, c[0]])[:, :, None], (2, D_MODEL, LANES))
    mods = _ada_call(c_rep, w_ada, b_ada).reshape(DEPTH, 2, 6, D_MODEL)

    w_rows = w_in.astype(BF16)
    w_t = _wt_call(w_in)
    tabs_t = tuple(t.T for t in tabs)
    col_rep = lambda a: jnp.broadcast_to(a[:, :, None], (DEPTH, HEAD_DIM, TM))
    gq_t, ga_t, gb_t = col_rep(a_q_norm), col_rep(a_out_norm), col_rep(b_out_norm)
    w_out_b = w_out.astype(BF16)
    ws_b = c_spatial.astype(BF16)
    bs_rep = jnp.broadcast_to(c_spatial_bias[:, :, :, None], (DEPTH, C_GROUPS, CHUNK, LANES))
    wr_pad = jnp.pad(w_router, ((0, 0), (0, LANES - N_EXPERTS)))
    wr_hi = wr_pad.astype(BF16)
    wr_t = jnp.stack([wr_hi, (wr_pad - wr_hi.astype(F32)).astype(BF16)])
    rb_rep = jnp.broadcast_to(router_bias[:, None], (N_EXPERTS, TM))
    tri = jnp.triu(jnp.ones((TM, TM), BF16), k=1)

    xs = jnp.concatenate([ctx[0], x[0]], axis=0)
    for l in range(DEPTH):
        lam_init = 0.8 - 0.6 * math.exp(-0.3 * l)
        mod = mods[l]
        vec = lambda a: a[l].reshape(1, -1)
        qar, qaf, ka, va, qbr, qbf, kb, vb, yc = _inproj_call(
            l, xs, mod, w_rows, w_t, tabs, tabs_t, gq_t[l], vec(a_k_norm), ws_b[l], bs_rep[l], vec(c_out_norm))
        ya = _attn_call(qar, qaf, ka, va, ga_t[l], None, groups=A_GROUP, kv_heads=1, diff=False,
                        lam_init=lam_init, name="attn_gqa")
        yb = _attn_call(qbr, qbf, kb, vb, gb_t[l], b_lambda[l], groups=4, kv_heads=2, diff=True,
                        lam_init=lam_init, name="attn_diff")
        x1, h2, route, counts = _outproj_call(l, ya, yb, yc, w_out_b, xs, mod, vec(ln1_g), vec(ln1_b),
                                              wr_t, rb_rep, tri)

        cnt = counts[:, 0].astype(jnp.int32)
        padded = (cnt + MOE_BM - 1) // MOE_BM * MOE_BM
        ends = jnp.cumsum(padded)
        starts = ends - padded
        e1 = route[0].astype(jnp.int32)
        e2 = route[1].astype(jnp.int32)
        dest = jnp.concatenate([starts[e1] + route[2].astype(jnp.int32), starts[e2] + route[3].astype(jnp.int32)])
        n_active = (ends[-1] // MOE_BM).astype(jnp.int32)
        blk = jnp.minimum(jnp.arange(MOE_NB, dtype=jnp.int32), n_active - 1)
        block_e = jnp.minimum(jnp.sum(ends[None, :] <= (blk * MOE_BM)[:, None], axis=1), N_EXPERTS - 1).astype(jnp.int32)

        first = (jnp.arange(MOE_NB) < n_active) & (block_e != jnp.concatenate([jnp.full((1,), -1, jnp.int32),
                                                                                 block_e[:-1]]))
        w_slot = (jnp.cumsum(first) - 1) % 2
        e_idx = jnp.arange(N_EXPERTS)
        later = jnp.where((cnt > 0)[None, :] & (e_idx[None, :] > e_idx[:, None]), e_idx[None, :], N_EXPERTS).min(axis=1)
        next_e = jnp.where(later == N_EXPERTS, -1, later)[block_e]
        plan = jnp.stack([first.astype(jnp.int32), w_slot.astype(jnp.int32), next_e.astype(jnp.int32)],
                         axis=1).reshape(-1)

        src = _slot_source_call(dest)
        ys = _moe_call(l, block_e, plan, n_active.reshape(1), src, h2, w_gate, w_up, w_down)
        xs = _combine_call(dest, ys, x1, route[4].reshape(TOK, 1), route[5].reshape(TOK, 1), mod,
                           vec(ln2_g), vec(ln2_b), first_tile=1 if l == DEPTH - 1 else 0)
    return xs.reshape(1, SEQ, D_MODEL)
```
